```python
import math
import jax
import jax.numpy as jnp
from jax import lax
import numpy as np

D_MODEL = 1024
BATCH = 8
SEQ = 4096
DEPTH = 1

CHUNK = 64
D_MIX = D_MODEL
HG_WIDTH = D_MIX // 2
HG_DK = 128
HG_HEADS = HG_WIDTH // HG_DK
HG_DV = HG_WIDTH // HG_HEADS
S5_WIDTH = D_MIX - HG_WIDTH
S5_GROUP = 16
S5_GROUPS = S5_WIDTH // S5_GROUP
S5_STATE = 64
S5_DT_MIN = 1e-3
S5_DT_MAX = 1e-1
IN_COLS = 4 * HG_WIDTH + S5_WIDTH
PEER_KEYS = 128
PEER_EXPERTS = PEER_KEYS * PEER_KEYS
PEER_HEADS = 8
PEER_TOPK = 16
PEER_QDIM = 256
PEER_HALF = PEER_QDIM // 2
PEER_TOKEN_BLOCK = 128
DN_ALPHA = (2.0 * DEPTH) ** 0.25
DN_BETA = (8.0 * DEPTH) ** -0.25
LN_EPS = 1e-5
RMS_EPS = 1e-6

kernel_name = 'hymba_hgrn2_s5_peer_deepnorm'


def layer_norm(x, g, b):
    x = x.astype(jnp.float32)
    mu = jnp.mean(x, axis=-1, keepdims=True)
    var = jnp.mean(jnp.square(x - mu), axis=-1, keepdims=True)
    return (x - mu) * lax.rsqrt(var + LN_EPS) * g + b


def hgrn2_heads(q, f_logit, v_in, gate, lower_bound, norm_g):
    bsz, seq, _ = q.shape
    n_chunks = seq // CHUNK
    forget = lower_bound + (1.0 - lower_bound) * jax.nn.sigmoid(f_logit)
    log_f = jnp.log(forget)
    k_in = 1.0 - forget

    def to_chunks(t, width):
        return t.reshape(bsz, n_chunks, CHUNK, HG_HEADS, width).transpose(1, 0, 3, 2, 4)

    qc = to_chunks(q * HG_DK ** -0.5, HG_DK)
    kc = to_chunks(k_in, HG_DK)
    gc = to_chunks(log_f, HG_DK)
    vc = to_chunks(v_in, HG_DV)
    causal = jnp.tril(jnp.ones((CHUNK, CHUNK), dtype=bool))[:, :, None]

    def chunk_step(state, blk):
        qb, kb, gb, vb = blk
        cum = jnp.cumsum(gb, axis=2)
        inter = jnp.einsum('bhtk,bhkv->bhtv', qb * jnp.exp(cum), state)
        diff = cum[:, :, :, None, :] - cum[:, :, None, :, :]
        decay = jnp.where(causal, jnp.exp(jnp.minimum(diff, 0.0)), 0.0)
        scores = jnp.einsum('bhtk,bhsk,bhtsk->bhts', qb, kb, decay)
        intra = jnp.einsum('bhts,bhsv->bhtv', scores, vb)
        last = cum[:, :, -1, :]
        new_state = jnp.exp(last)[..., None] * state + jnp.einsum(
            'bhsk,bhsv->bhkv', kb * jnp.exp(last[:, :, None, :] - cum), vb)
        return new_state, inter + intra

    state0 = jnp.zeros((bsz, HG_HEADS, HG_DK, HG_DV), jnp.float32)
    _, out = lax.scan(chunk_step, state0, (qc, kc, gc, vc))
    out = out.transpose(1, 0, 3, 2, 4).reshape(bsz, seq, HG_HEADS, HG_DV)
    rms = lax.rsqrt(jnp.mean(jnp.square(out), axis=-1, keepdims=True) + RMS_EPS)
    gate = gate.reshape(bsz, seq, HG_HEADS, HG_DV)
    out = out * rms * norm_g * jax.nn.silu(gate)
    return out.reshape(bsz, seq, HG_WIDTH)


def complex_affine_combine(e1, e2):
    a1r, a1i, b1r, b1i = e1
    a2r, a2i, b2r, b2i = e2
    return (a2r * a1r - a2i * a1i,
            a2r * a1i + a2i * a1r,
            a2r * b1r - a2i * b1i + b2r,
            a2r * b1i + a2i * b1r + b2i)


def s5_groups(u, a_re, a_im, log_step, b_re, b_im, c_re, c_im, d_skip, w_glu, b_glu):
    f32 = jnp.float32
    a_re, a_im = a_re.astype(f32), a_im.astype(f32)
    b_re, b_im = b_re.astype(f32), b_im.astype(f32)
    bsz, seq, _ = u.shape
    ug = u.reshape(bsz, seq, S5_GROUPS, S5_GROUP)
    step = jnp.exp(log_step.astype(f32))[:, None]
    mag = jnp.exp(step * a_re)
    abar_re = mag * jnp.cos(step * a_im)
    abar_im = mag * jnp.sin(step * a_im)
    num_re, num_im = abar_re - 1.0, abar_im
    den = jnp.square(a_re) + jnp.square(a_im)
    coef_re = ((num_re * a_re + num_im * a_im) / den)[..., None]
    coef_im = ((num_im * a_re - num_re * a_im) / den)[..., None]
    bbar_re = coef_re * b_re - coef_im * b_im
    bbar_im = coef_re * b_im + coef_im * b_re
    bu_re = jnp.einsum('bsgp,gnp->bsgn', ug, bbar_re)
    bu_im = jnp.einsum('bsgp,gnp->bsgn', ug, bbar_im)
    ar = jnp.broadcast_to(abar_re, (1, seq, S5_GROUPS, S5_STATE))
    ai = jnp.broadcast_to(abar_im, (1, seq, S5_GROUPS, S5_STATE))
    _, _, x_re, x_im = lax.associative_scan(
        complex_affine_combine, (ar, ai, bu_re, bu_im), axis=1)
    y = (jnp.einsum('bsgn,gpn->bsgp', x_re, c_re)
         - jnp.einsum('bsgn,gpn->bsgp', x_im, c_im)
         + d_skip * ug)
    y = y.reshape(bsz, seq, S5_WIDTH)
    z = jax.nn.gelu(y)
    return z * jax.nn.sigmoid(jnp.einsum('bsc,cd->bsd', z, w_glu) + b_glu)


def peer_ffn(h, w_query, keys_1, keys_2, w_down, w_up):
    bsz, seq, dim = h.shape
    tokens = h.reshape(-1, PEER_TOKEN_BLOCK, dim)

    def block(xb):
        q = jnp.einsum('td,dq->tq', xb, w_query).reshape(
            PEER_TOKEN_BLOCK, PEER_HEADS, 2, PEER_HALF)
        s1 = jnp.einsum('thd,nd->thn', q[:, :, 0], keys_1)
        s2 = jnp.einsum('thd,nd->thn', q[:, :, 1], keys_2)
        v1, i1 = lax.top_k(s1, PEER_TOPK)
        v2, i2 = lax.top_k(s2, PEER_TOPK)
        cand = (v1[..., :, None] + v2[..., None, :]).reshape(
            PEER_TOKEN_BLOCK, PEER_HEADS, PEER_TOPK * PEER_TOPK)
        cidx = (i1[..., :, None] * PEER_KEYS + i2[..., None, :]).reshape(
            PEER_TOKEN_BLOCK, PEER_HEADS, PEER_TOPK * PEER_TOPK)
        best, pos = lax.top_k(cand, PEER_TOPK)
        expert = jnp.take_along_axis(cidx, pos, axis=-1)
        gate = jax.nn.softmax(best.astype(jnp.float32), axis=-1)
        u = jnp.take(w_down, expert, axis=0)
        act = jax.nn.gelu(jnp.einsum('td,thkd->thk', xb, u)) * gate
        v = jnp.take(w_up, expert, axis=0)
        return jnp.einsum('thk,thkd->td', act, v)

    out = lax.map(block, tokens)
    return out.reshape(bsz, seq, dim)


def setup_inputs(seed: int = 0) -> dict:
    key = jax.random.key(seed)
    ks = jax.random.split(key, 32)
    f32 = jnp.float32

    def nrm(k, shape, scale):
        return jax.random.normal(k, shape, f32) * scale

    col_scale = jnp.concatenate([
        jnp.ones((2 * HG_WIDTH,), f32),
        jnp.full((HG_WIDTH,), DN_BETA, f32),
        jnp.ones((HG_WIDTH,), f32),
        jnp.full((S5_WIDTH,), DN_BETA, f32)])
    n_idx = jnp.arange(S5_STATE, dtype=f32)
    return {
        'x': nrm(ks[0], (BATCH, SEQ, D_MODEL), 1.0),
        'ln0_g': 1.0 + nrm(ks[1], (D_MODEL,), 0.02),
        'ln0_b': nrm(ks[2], (D_MODEL,), 0.02),
        'w_in': nrm(ks[3], (DEPTH, D_MODEL, IN_COLS), D_MODEL ** -0.5) * col_scale,
        'hg_lb_logits': nrm(ks[4], (DEPTH + 1, HG_WIDTH), 1.0),
        'hg_norm_g': 1.0 + nrm(ks[5], (DEPTH, HG_DV), 0.02),
        's5_a_re': -0.5 + nrm(ks[6], (DEPTH, S5_GROUPS, S5_STATE), 0.01),
        's5_a_im': jnp.pi * n_idx + nrm(ks[7], (DEPTH, S5_GROUPS, S5_STATE), 0.01),
        's5_log_step': jax.random.uniform(ks[8], (DEPTH, S5_GROUPS), f32,
                                          math.log(S5_DT_MIN), math.log(S5_DT_MAX)),
        's5_b_re': nrm(ks[9], (DEPTH, S5_GROUPS, S5_STATE, S5_GROUP), (2.0 * S5_GROUP) ** -0.5),
        's5_b_im': nrm(ks[10], (DEPTH, S5_GROUPS, S5_STATE, S5_GROUP), (2.0 * S5_GROUP) ** -0.5),
        's5_c_re': nrm(ks[11], (DEPTH, S5_GROUPS, S5_GROUP, S5_STATE), (2.0 * S5_STATE) ** -0.5),
        's5_c_im': nrm(ks[12], (DEPTH, S5_GROUPS, S5_GROUP, S5_STATE), (2.0 * S5_STATE) ** -0.5),
        's5_d': nrm(ks[13], (DEPTH, S5_GROUPS, S5_GROUP), 1.0),
        'w_glu': nrm(ks[14], (DEPTH, S5_WIDTH, S5_WIDTH), S5_WIDTH ** -0.5),
        'b_glu': nrm(ks[15], (DEPTH, S5_WIDTH), 0.02),
        'w_out': nrm(ks[16], (DEPTH, D_MIX, D_MODEL), DN_BETA * D_MIX ** -0.5),
        'ln1_g': 1.0 + nrm(ks[17], (DEPTH, D_MODEL), 0.02),
        'ln1_b': nrm(ks[18], (DEPTH, D_MODEL), 0.02),
        'w_query': nrm(ks[19], (DEPTH, D_MODEL, PEER_HEADS * PEER_QDIM), D_MODEL ** -0.5),
        'peer_keys_1': nrm(ks[20], (DEPTH, PEER_KEYS, PEER_HALF), PEER_HALF ** -0.5),
        'peer_keys_2': nrm(ks[21], (DEPTH, PEER_KEYS, PEER_HALF), PEER_HALF ** -0.5),
        'peer_down': nrm(ks[22], (DEPTH, PEER_EXPERTS, D_MODEL), D_MODEL ** -0.5),
        'peer_up': nrm(ks[23], (DEPTH, PEER_EXPERTS, D_MODEL), DN_BETA * PEER_HEADS ** -0.5),
        'ln2_g': 1.0 + nrm(ks[24], (DEPTH, D_MODEL), 0.02),
        'ln2_b': nrm(ks[25], (DEPTH, D_MODEL), 0.02),
    }


def reference(x, ln0_g, ln0_b, w_in, hg_lb_logits, hg_norm_g, s5_a_re, s5_a_im,
              s5_log_step, s5_b_re, s5_b_im, s5_c_re, s5_c_im, s5_d, w_glu, b_glu,
              w_out, ln1_g, ln1_b, w_query, peer_keys_1, peer_keys_2, peer_down,
              peer_up, ln2_g, ln2_b):
    h = layer_norm(x, ln0_g, ln0_b)
    lower_bounds = jnp.cumsum(
        jax.nn.softmax(hg_lb_logits.astype(jnp.float32), axis=0), axis=0)
    for layer in range(DEPTH):
        proj = jnp.einsum('bsd,dc->bsc', h, w_in[layer])
        q, f_logit, v_in, gate, u = jnp.split(
            proj, [HG_WIDTH, 2 * HG_WIDTH, 3 * HG_WIDTH, 4 * HG_WIDTH], axis=-1)
        mix_a = hgrn2_heads(q, f_logit, v_in, gate, lower_bounds[layer], hg_norm_g[layer])
        mix_b = s5_groups(u, s5_a_re[layer], s5_a_im[layer], s5_log_step[layer],
                          s5_b_re[layer], s5_b_im[layer], s5_c_re[layer], s5_c_im[layer],
                          s5_d[layer], w_glu[layer], b_glu[layer])
        mixed = jnp.einsum('bsc,cd->bsd', jnp.concatenate([mix_a, mix_b], axis=-1),
                           w_out[layer])
        h = layer_norm(DN_ALPHA * h + mixed, ln1_g[layer], ln1_b[layer])
        ffn = peer_ffn(h, w_query[layer], peer_keys_1[layer], peer_keys_2[layer],
                       peer_down[layer], peer_up[layer])
        h = layer_norm(DN_ALPHA * h + ffn, ln2_g[layer], ln2_b[layer])
    return h.astype(x.dtype)
```

```python
import functools
import math

import jax
import jax.numpy as jnp
import numpy as np
from jax import lax
from jax.experimental import pallas as pl
from jax.experimental.pallas import tpu as pltpu

F32 = jnp.float32
BF16 = jnp.bfloat16

CHUNK = 64
SUB = 16
HG_DK = 128
HG_HEADS = 4
S5_GROUP = 16
S5_STATE = 64
PEER_KEYS = 128
PEER_HEADS = 8
PEER_TOPK = 16
PEER_HALF = 128
LN_EPS = 1e-5
RMS_EPS = 1e-6
DN_ALPHA = 2.0 ** 0.25
NEG_INF = float("-inf")
LANES = 128

VMEM_LIMIT_BYTES = 56 * 1024 * 1024


def _params(*semantics):
    return pltpu.CompilerParams(dimension_semantics=semantics,
                                vmem_limit_bytes=VMEM_LIMIT_BYTES)


def _bf(x):
    return x.astype(BF16)


def _layer_norm(x, g, b):
    mu = jnp.mean(x, axis=-1, keepdims=True)
    xc = x - mu
    var = jnp.mean(xc * xc, axis=-1, keepdims=True)
    return xc * lax.rsqrt(var + LN_EPS) * g + b


def _sigmoid(x):
    return 1.0 / (1.0 + jnp.exp(-x))


def _gelu(x):
    c = math.sqrt(2.0 / math.pi)
    return 0.5 * x * (1.0 + jnp.tanh(c * (x + 0.044715 * (x * x * x))))


def _inproj_kernel(x_ref, g_ref, b_ref, w_ref, lbl_ref,
                   h_ref, q_ref, gl_ref, k_ref, v_ref, gate_ref, u_ref, *, hw):
    h = _layer_norm(x_ref[...], g_ref[...], b_ref[...])
    h_ref[...] = h
    proj = jnp.dot(_bf(h), w_ref[...], preferred_element_type=F32)
    logits = lbl_ref[...]
    e = jnp.exp(logits - jnp.max(logits, axis=0, keepdims=True))
    lb = e[0:1, :] / jnp.sum(e, axis=0, keepdims=True)
    forget = lb + (1.0 - lb) * _sigmoid(proj[:, hw:2 * hw])
    q_ref[...] = proj[:, 0:hw] * (HG_DK ** -0.5)
    gl_ref[...] = jnp.log(forget)
    k_ref[...] = 1.0 - forget
    v_ref[...] = proj[:, 2 * hw:3 * hw]
    gate_ref[...] = proj[:, 3 * hw:4 * hw]
    u_ref[...] = proj[:, 4 * hw:]


def _inproj(x2, ln_g, ln_b, w_in_bf, lb_logits, *, tm):
    n, d = x2.shape
    cols = w_in_bf.shape[1]
    hw = lb_logits.shape[1]
    uw = cols - 4 * hw
    row = lambda w: pl.BlockSpec((tm, w), lambda i: (i, 0))
    full = lambda a: pl.BlockSpec(a.shape, lambda i: (0,) * a.ndim)
    outs = [jax.ShapeDtypeStruct((n, d), F32)] + \
           [jax.ShapeDtypeStruct((n, hw), F32)] * 5 + [jax.ShapeDtypeStruct((n, uw), F32)]
    return pl.pallas_call(
        functools.partial(_inproj_kernel, hw=hw),
        grid=(n // tm,),
        in_specs=[row(d), full(ln_g), full(ln_b), full(w_in_bf), full(lb_logits)],
        out_specs=[row(d)] + [row(hw)] * 5 + [row(uw)],
        out_shape=outs,
        compiler_params=_params("parallel"),
        name="inproj",
    )(x2, ln_g, ln_b, w_in_bf, lb_logits)


def _hgrn_masks(tc):
    t = np.arange(tc)
    le = t[None, :] <= t[:, None]
    same_chunk = (t[None, :] // CHUNK) == (t[:, None] // CHUNK)
    same_sub = (t[None, :] // SUB) == (t[:, None] // SUB)
    m = np.concatenate([le & same_chunk, le & same_sub, same_sub], axis=0)
    return m.astype(np.float32)


def _diag_selector():
    s = np.repeat(np.arange(SUB), HG_DK)
    c = np.arange(CHUNK)
    return (s[:, None] == (c[None, :] % SUB)).astype(np.float32)


def _hgrn_kernel(q_ref, g_ref, k_ref, v_ref, gate_ref, ng_ref, m3_ref, sel_ref,
                 o_ref, state_ref, *, tc):
    nch = tc // CHUNK
    nsub = tc // SUB

    @pl.when(pl.program_id(1) == 0)
    def _():
        state_ref[...] = jnp.zeros_like(state_ref)

    sums = jnp.dot(m3_ref[...], g_ref[...], preferred_element_type=F32,
                   precision=lax.Precision.HIGHEST)
    row = lax.broadcasted_iota(jnp.int32, (CHUNK, CHUNK), 0)
    col = lax.broadcasted_iota(jnp.int32, (CHUNK, CHUNK), 1)
    blk = row // SUB - col // SUB
    diag_ok = (blk == 0) & (col <= row)
    sel = sel_ref[...]
    ng = ng_ref[...]

    for h in range(HG_HEADS):
        lanes = slice(h * HG_DK, (h + 1) * HG_DK)
        q = q_ref[:, lanes]
        k = k_ref[:, lanes]
        v = v_ref[:, lanes]
        cum = sums[0:tc, lanes]
        cs = sums[tc:2 * tc, lanes]
        tot = sums[2 * tc:3 * tc, lanes]
        cum3 = cum.reshape(nch, CHUNK, HG_DK)
        last = cum3[:, CHUNK - 1:CHUNK, :]
        q_inter = _bf(q * jnp.exp(cum))
        k_state = _bf((k.reshape(nch, CHUNK, HG_DK) * jnp.exp(last - cum3)).reshape(tc, HG_DK))
        v_bf = _bf(v)
        q_sub = q * jnp.exp(cs)
        k_sub = _bf(k * jnp.exp(tot - cs))
        zeros1 = jnp.zeros((SUB, HG_DK), F32)
        tot_p1 = jnp.concatenate([zeros1, tot[:tc - SUB]], axis=0)
        tot_p2 = jnp.concatenate([zeros1, zeros1, tot[:tc - 2 * SUB]], axis=0)
        q_sub2 = q_sub * jnp.exp(tot_p1)
        q_sub3 = q_sub2 * jnp.exp(tot_p2)
        cs3 = cs.reshape(nsub, SUB, HG_DK)
        q3 = q.reshape(nsub, SUB, HG_DK)
        k3 = k.reshape(nsub, SUB, HG_DK)
        pieces = []
        for s in range(SUB):
            decay = jnp.exp(jnp.minimum(cs3 - cs3[:, s:s + 1, :], 0.0))
            pieces.append(_bf((q3 * decay * k3[:, s:s + 1, :]).reshape(tc, HG_DK)))
        diag = jnp.dot(jnp.concatenate(pieces, axis=1), sel,
                       preferred_element_type=F32)

        st = state_ref[h]
        outs = []
        for c in range(nch):
            rows = slice(c * CHUNK, (c + 1) * CHUNK)
            inter = lax.dot_general(q_inter[rows], _bf(st), (((1,), (1,)), ((), ())),
                                    preferred_element_type=F32)
            lhs = _bf(jnp.concatenate([q_sub[rows], q_sub2[rows], q_sub3[rows]], axis=0))
            off = lax.dot_general(lhs, k_sub[rows], (((1,), (1,)), ((), ())),
                                  preferred_element_type=F32)
            sc = jnp.where(diag_ok, diag[rows],
                           jnp.where(blk == 1, off[0:CHUNK],
                                     jnp.where(blk == 2, off[CHUNK:2 * CHUNK],
                                               jnp.where(blk == 3, off[2 * CHUNK:], 0.0))))
            intra = jnp.dot(_bf(sc), v_bf[rows], preferred_element_type=F32)
            outs.append(inter + intra)
            upd = lax.dot_general(v_bf[rows], k_state[rows], (((0,), (0,)), ((), ())),
                                  preferred_element_type=F32)
            st = st * jnp.exp(last[c]) + upd
        state_ref[h] = st
        o = jnp.concatenate(outs, axis=0)
        rms = lax.rsqrt(jnp.mean(o * o, axis=-1, keepdims=True) + RMS_EPS)
        gate = gate_ref[:, lanes]
        o_ref[:, lanes] = o * rms * ng * (gate * _sigmoid(gate))


def _hgrn(q, gl, k, v, gate, norm_g, *, batch, tc):
    n, hw = q.shape
    nblk = n // batch // tc
    m3 = jnp.asarray(_hgrn_masks(tc))
    sel = jnp.asarray(_diag_selector(), dtype=BF16)
    row = pl.BlockSpec((tc, hw), lambda b, s: (b * nblk + s, 0))
    full = lambda a: pl.BlockSpec(a.shape, lambda b, s: (0,) * a.ndim)
    return pl.pallas_call(
        functools.partial(_hgrn_kernel, tc=tc),
        grid=(batch, nblk),
        in_specs=[row] * 5 + [full(norm_g), full(m3), full(sel)],
        out_specs=row,
        out_shape=jax.ShapeDtypeStruct((n, hw), F32),
        scratch_shapes=[pltpu.VMEM((HG_HEADS, HG_DK, HG_DK), F32)],
        compiler_params=_params("parallel", "arbitrary"),
        name="hgrn2",
    )(q, gl, k, v, gate, norm_g, m3, sel)


def _s5_kernel(u_ref, ar_ref, ai_ref, wb_ref, wc_ref, d_ref, wg_ref, bg_ref,
               o_ref, bu_ref, st_ref, *, tt):
    batch, _, width = u_ref.shape
    ns = ar_ref.shape[1]

    @pl.when(pl.program_id(0) == 0)
    def _():
        st_ref[...] = jnp.zeros_like(st_ref)

    nlb = ns // LANES
    u2 = u_ref[...].reshape(batch * tt, width)
    bu = jnp.dot(_bf(u2), wb_ref[...], preferred_element_type=F32)
    for j in range(2 * nlb):
        bu_ref[j] = bu[:, j * LANES:(j + 1) * LANES]
    ar = ar_ref[...]
    ai = ai_ref[...]

    def step(t, carry):
        xr, xi = carry
        rows = pl.ds(t, batch, stride=tt)
        nr, ni = [], []
        for j in range(nlb):
            lanes = slice(j * LANES, (j + 1) * LANES)
            r = ar[:, lanes] * xr[:, lanes] - ai[:, lanes] * xi[:, lanes] + bu_ref[j, rows, :]
            i = ar[:, lanes] * xi[:, lanes] + ai[:, lanes] * xr[:, lanes] + bu_ref[nlb + j, rows, :]
            bu_ref[j, rows, :] = r
            bu_ref[nlb + j, rows, :] = i
            nr.append(r)
            ni.append(i)
        return jnp.concatenate(nr, axis=1), jnp.concatenate(ni, axis=1)

    st = st_ref[...]
    xr, xi = lax.fori_loop(0, tt, step, (st[:, :ns], st[:, ns:]))
    st_ref[...] = jnp.concatenate([xr, xi], axis=1)

    states = jnp.concatenate([bu_ref[j] for j in range(2 * nlb)], axis=1)
    y = jnp.dot(_bf(states), wc_ref[...], preferred_element_type=F32) + d_ref[...] * u2
    z = _gelu(y)
    gl = jnp.dot(_bf(z), wg_ref[...], preferred_element_type=F32) + bg_ref[...]
    o_ref[...] = (z * _sigmoid(gl)).reshape(batch, tt, width)


def _s5(u3, abar_re, abar_im, wb, wc, d, wg, bg, *, tt):
    batch, seq, width = u3.shape
    ns = abar_re.shape[1]
    blk = pl.BlockSpec((batch, tt, width), lambda s: (0, s, 0))
    full = lambda a: pl.BlockSpec(a.shape, lambda s: (0,) * a.ndim)
    return pl.pallas_call(
        functools.partial(_s5_kernel, tt=tt),
        grid=(seq // tt,),
        in_specs=[blk] + [full(a) for a in (abar_re, abar_im, wb, wc, d, wg, bg)],
        out_specs=blk,
        out_shape=jax.ShapeDtypeStruct(u3.shape, F32),
        scratch_shapes=[pltpu.VMEM((2 * ns // LANES, batch * tt, LANES), F32),
                        pltpu.VMEM((batch, 2 * ns), F32)],
        compiler_params=_params("arbitrary"),
        name="s5",
    )(u3, abar_re, abar_im, wb, wc, d, wg, bg)


def _s5_discretise(a_re, a_im, log_step, b_re, b_im, c_re, c_im):
    groups, nstate, p = b_re.shape
    step = jnp.exp(log_step.astype(F32))[:, None]
    mag = jnp.exp(step * a_re)
    abar_re = mag * jnp.cos(step * a_im)
    abar_im = mag * jnp.sin(step * a_im)
    num_re, num_im = abar_re - 1.0, abar_im
    den = jnp.square(a_re) + jnp.square(a_im)
    coef_re = ((num_re * a_re + num_im * a_im) / den)[..., None]
    coef_im = ((num_im * a_re - num_re * a_im) / den)[..., None]
    bbar_re = coef_re * b_re - coef_im * b_im
    bbar_im = coef_re * b_im + coef_im * b_re
    eye = jnp.eye(groups, dtype=F32)
    bd = lambda m: jnp.einsum("gnp,gh->gphn", m, eye).reshape(groups * p, groups * nstate)
    wb = jnp.concatenate([bd(bbar_re), bd(bbar_im)], axis=1)
    cd = lambda m: jnp.einsum("gpn,gh->gnhp", m, eye).reshape(groups * nstate, groups * p)
    wc = jnp.concatenate([cd(c_re), -cd(c_im)], axis=0)
    flat = lambda m: m.reshape(1, groups * nstate)
    return flat(abar_re), flat(abar_im), _bf(wb), _bf(wc)


def _extract_top(s, want_rank):
    vals = []
    rank = jnp.full(s.shape, float(PEER_TOPK), F32)
    cur = s
    for r in range(PEER_TOPK):
        m = jnp.max(cur, axis=0, keepdims=True)
        vals.append(m)
        hit = cur == m
        if want_rank:
            rank = jnp.where(hit, float(r), rank)
        cur = jnp.where(hit, NEG_INF, cur)
    return vals, rank


def _stack_rows(rows, nrows):
    t = rows[0].shape[1]
    idx = lax.broadcasted_iota(jnp.int32, (nrows, t), 0)
    out = jnp.zeros((nrows, t), F32)
    for r, v in enumerate(rows):
        out = jnp.where(idx == r, v, out)
    return out


def _route_head(s1, s2):
    v1, _ = _extract_top(s1, False)
    v2, rank2 = _extract_top(s2, True)
    t = s1.shape[1]
    v2_lo = _stack_rows(v2[:8], 8)
    v2_hi = _stack_rows(v2[8:], 8)
    v1_hi = _stack_rows(v1[8:], 8)
    idx8 = lax.broadcasted_iota(jnp.int32, (8, t), 0)
    cand = [v1[0] + v2_lo, v1[0] + v2_hi]
    for a in range(1, 8):
        nb = PEER_TOPK // (a + 1)
        c = v1[a] + v2_lo
        cand.append(c if nb >= 8 else jnp.where(idx8 < nb, c, NEG_INF))
    cand.append(v1_hi + v2[0])
    cur = jnp.concatenate(cand, axis=0)
    top = v1[0] + v2[0]
    z = jnp.zeros_like(top)
    tau = top
    for _ in range(PEER_TOPK):
        tau = jnp.max(cur, axis=0, keepdims=True)
        z = z + jnp.exp(tau - top)
        cur = jnp.where(cur == tau, NEG_INF, cur)
    cnt = jnp.zeros(s1.shape, F32)
    for b in range(PEER_TOPK):
        cnt = cnt + jnp.where(s1 + v2[b] >= tau, 1.0, 0.0)
    e1 = jnp.exp(s1 - v1[0])
    e2 = jnp.exp(s2 - v2[0]) / z
    return e1, cnt, e2, rank2


def _route_kernel(ma_ref, mb_ref, h0_ref, wo_ref, g_ref, b_ref, wq_ref, k1_ref, k2_ref,
                  h1t_ref, h1tb_ref, e1_ref, cn_ref, e2_ref, r2_ref):
    hw = ma_ref.shape[1]
    mixed = jnp.dot(_bf(ma_ref[...]), wo_ref[0:hw, :], preferred_element_type=F32)
    mixed = mixed + jnp.dot(_bf(mb_ref[...]), wo_ref[hw:, :], preferred_element_type=F32)
    h1 = _layer_norm(DN_ALPHA * h0_ref[...] + mixed, g_ref[...], b_ref[...])
    h1t = h1.T
    h1t_ref[...] = h1t
    h1tb = _bf(h1t)
    h1tb_ref[...] = h1tb
    qt = jnp.dot(wq_ref[...], h1tb, preferred_element_type=F32)
    k1 = k1_ref[...]
    k2 = k2_ref[...]
    e1s, cns = [], []
    for h in range(PEER_HEADS):
        base = h * 2 * PEER_HALF
        s1 = jnp.dot(k1, _bf(qt[base:base + PEER_HALF]), preferred_element_type=F32)
        s2 = jnp.dot(k2, _bf(qt[base + PEER_HALF:base + 2 * PEER_HALF]),
                     preferred_element_type=F32)
        e1, cnt, e2, rank2 = _route_head(s1, s2)
        e1_ref[h] = e1
        cn_ref[h] = cnt
        e2_ref[h] = _bf(e2)
        r2_ref[h] = _bf(rank2)


def _route(mix_a, mix_b, h0, w_out_bf, ln_g, ln_b, wq_t_bf, k1_bf, k2_bf, *, tm):
    n, d = h0.shape
    hw = mix_a.shape[1]
    row = lambda w: pl.BlockSpec((tm, w), lambda i: (i, 0))
    full = lambda a: pl.BlockSpec(a.shape, lambda i: (0,) * a.ndim)
    colblk = pl.BlockSpec((d, tm), lambda i: (0, i))
    tab = pl.BlockSpec((PEER_HEADS, PEER_KEYS, tm), lambda i: (0, 0, i))
    tab_shape = (PEER_HEADS, PEER_KEYS, n)
    return pl.pallas_call(
        _route_kernel,
        grid=(n // tm,),
        in_specs=[row(hw), row(hw), row(d)] +
                 [full(a) for a in (w_out_bf, ln_g, ln_b, wq_t_bf, k1_bf, k2_bf)],
        out_specs=[colblk, colblk, tab, tab, tab, tab],
        out_shape=[jax.ShapeDtypeStruct((d, n), F32), jax.ShapeDtypeStruct((d, n), BF16),
                   jax.ShapeDtypeStruct(tab_shape, F32), jax.ShapeDtypeStruct(tab_shape, F32),
                   jax.ShapeDtypeStruct(tab_shape, BF16), jax.ShapeDtypeStruct(tab_shape, BF16)],
        compiler_params=_params("parallel"),
        name="route",
    )(mix_a, mix_b, h0, w_out_bf, ln_g, ln_b, wq_t_bf, k1_bf, k2_bf)


def _experts_kernel(wd_ref, wut_ref, h1tb_ref, h1t_ref, e1_ref, cn_ref, e2_ref, r2_ref,
                    g_ref, b_ref, o_ref, acc_ref, p_ref, *, ib):
    kstep = pl.program_id(1)

    @pl.when(kstep == 0)
    def _():
        acc_ref[...] = jnp.zeros_like(acc_ref)

    a = jnp.dot(wd_ref[...], h1tb_ref[...], preferred_element_type=F32)
    tt = a.shape[1]
    for ii in range(ib):
        gsum = jnp.zeros((PEER_KEYS, tt), BF16)
        for h in range(PEER_HEADS):
            e1 = _bf(jnp.broadcast_to(e1_ref[h, ii:ii + 1, :], (PEER_KEYS, tt)))
            cn = _bf(jnp.broadcast_to(cn_ref[h, ii:ii + 1, :], (PEER_KEYS, tt)))
            val = e2_ref[h] * e1
            gsum = gsum + jnp.where(r2_ref[h] < cn, val, jnp.zeros_like(val))
        rows = slice(ii * PEER_KEYS, (ii + 1) * PEER_KEYS)
        p_ref[rows, :] = _bf(_gelu(a[rows])) * gsum
    acc_ref[...] += jnp.dot(wut_ref[...], p_ref[...], preferred_element_type=F32)

    @pl.when(kstep == pl.num_programs(1) - 1)
    def _():
        y = DN_ALPHA * h1t_ref[...] + acc_ref[...]
        mu = jnp.mean(y, axis=0, keepdims=True)
        yc = y - mu
        var = jnp.mean(yc * yc, axis=0, keepdims=True)
        yn = yc * lax.rsqrt(var + LN_EPS)
        o_ref[...] = yn.T * g_ref[...] + b_ref[...]


def _experts(wd_bf, wut_bf, h1tb, h1t, e1, cn, e2, r2, ln_g, ln_b, *, tt, ib):
    d, n = h1t.shape
    nexp = wd_bf.shape[0]
    eb = ib * PEER_KEYS
    tokcol = pl.BlockSpec((d, tt), lambda j, k: (0, j))
    tab_i = pl.BlockSpec((PEER_HEADS, ib, tt), lambda j, k: (0, k, j))
    tab_full = pl.BlockSpec((PEER_HEADS, PEER_KEYS, tt), lambda j, k: (0, 0, j))
    full = lambda a: pl.BlockSpec(a.shape, lambda j, k: (0,) * a.ndim)
    return pl.pallas_call(
        functools.partial(_experts_kernel, ib=ib),
        grid=(n // tt, nexp // eb),
        in_specs=[pl.BlockSpec((eb, d), lambda j, k: (k, 0)),
                  pl.BlockSpec((d, eb), lambda j, k: (0, k)),
                  tokcol, tokcol, tab_i, tab_i, tab_full, tab_full, full(ln_g), full(ln_b)],
        out_specs=pl.BlockSpec((tt, d), lambda j, k: (j, 0)),
        out_shape=jax.ShapeDtypeStruct((n, d), F32),
        scratch_shapes=[pltpu.VMEM((d, tt), F32), pltpu.VMEM((eb, tt), BF16)],
        compiler_params=_params("parallel", "arbitrary"),
        name="experts",
    )(wd_bf, wut_bf, h1tb, h1t, e1, cn, e2, r2, ln_g, ln_b)


def _tile(n, pref):
    return pref if n % pref == 0 else n


def kernel(x, ln0_g, ln0_b, w_in, hg_lb_logits, hg_norm_g, s5_a_re, s5_a_im, s5_log_step,
           s5_b_re, s5_b_im, s5_c_re, s5_c_im, s5_d, w_glu, b_glu, w_out, ln1_g, ln1_b,
           w_query, peer_keys_1, peer_keys_2, peer_down, peer_up, ln2_g, ln2_b):
    assert w_in.shape[0] == 1, "single-layer trunk"
    batch, seq, d = x.shape
    n = batch * seq
    r2d = lambda a: a.reshape(1, -1).astype(F32)

    x2 = x.reshape(n, d).astype(F32)
    h0, q, gl, k, v, gate, u = _inproj(
        x2, r2d(ln0_g), r2d(ln0_b), _bf(w_in[0]), hg_lb_logits.astype(F32),
        tm=_tile(n, 512))

    mix_a = _hgrn(q, gl, k, v, gate, r2d(hg_norm_g[0]), batch=batch, tc=_tile(seq, 512))

    abar_re, abar_im, wb, wc = _s5_discretise(
        s5_a_re[0].astype(F32), s5_a_im[0].astype(F32), s5_log_step[0],
        s5_b_re[0].astype(F32), s5_b_im[0].astype(F32), s5_c_re[0], s5_c_im[0])
    mix_b = _s5(u.reshape(batch, seq, -1), abar_re, abar_im, wb, wc, r2d(s5_d[0]),
                _bf(w_glu[0]), r2d(b_glu[0]), tt=_tile(seq, 64)).reshape(n, -1)

    h1t, h1tb, e1, cn, e2, r2 = _route(
        mix_a, mix_b, h0, _bf(w_out[0]), r2d(ln1_g[0]), r2d(ln1_b[0]),
        _bf(w_query[0].T), _bf(peer_keys_1[0]), _bf(peer_keys_2[0]), tm=_tile(n, 256))

    out = _experts(_bf(peer_down[0]), _bf(peer_up[0].T), h1tb, h1t, e1, cn, e2, r2,
                   r2d(ln2_g[0]), r2d(ln2_b[0]), tt=_tile(n, 512), ib=8)
    return out.reshape(batch, seq, d).astype(x.dtype)
```

```python
import functools
import math

import jax
import jax.numpy as jnp
import numpy as np
from jax import lax
from jax.experimental import pallas as pl
from jax.experimental.pallas import tpu as pltpu

F32 = jnp.float32
BF16 = jnp.bfloat16

CHUNK = 64
SUB = 16
HG_DK = 128
HG_HEADS = 4
S5_GROUP = 16
S5_STATE = 64
PEER_KEYS = 128
PEER_HEADS = 8
PEER_TOPK = 16
PEER_HALF = 128
LN_EPS = 1e-5
RMS_EPS = 1e-6
DN_ALPHA = 2.0 ** 0.25
NEG_INF = float("-inf")
LANES = 128
LOG2E = math.log2(math.e)

VMEM_LIMIT_BYTES = 56 * 1024 * 1024


def _params(*semantics, flags=None):
    return pltpu.CompilerParams(dimension_semantics=semantics,
                                vmem_limit_bytes=VMEM_LIMIT_BYTES, flags=flags)


def _bf(x):
    return x.astype(BF16)


def _layer_norm(x, g, b):
    mu = jnp.mean(x, axis=-1, keepdims=True)
    xc = x - mu
    var = jnp.mean(xc * xc, axis=-1, keepdims=True)
    return xc * lax.rsqrt(var + LN_EPS) * g + b


def _sigmoid(x):
    return 1.0 / (1.0 + jnp.exp(-x))


def _gelu(x):
    k0 = -2.0 * math.sqrt(2.0 / math.pi) * math.log2(math.e)
    w = x * (x * x * (k0 * 0.044715) + k0)
    return x / (1.0 + jnp.exp2(w))


def _inproj_kernel(x_ref, g_ref, b_ref, w_ref, lbl_ref,
                   h_ref, q_ref, gl_ref, k_ref, v_ref, gate_ref, u_ref, *, hw):
    h = _layer_norm(x_ref[...], g_ref[...], b_ref[...])
    h_ref[...] = h
    proj = jnp.dot(_bf(h), w_ref[...], preferred_element_type=F32)
    logits = lbl_ref[...]
    e = jnp.exp(logits - jnp.max(logits, axis=0, keepdims=True))
    lb = e[0:1, :] / jnp.sum(e, axis=0, keepdims=True)
    forget = lb + (1.0 - lb) * _sigmoid(proj[:, hw:2 * hw])
    q_ref[...] = proj[:, 0:hw] * (HG_DK ** -0.5)
    gl_ref[...] = jnp.log(forget)
    k_ref[...] = 1.0 - forget
    v_ref[...] = proj[:, 2 * hw:3 * hw]
    gate_ref[...] = proj[:, 3 * hw:4 * hw]
    u_ref[...] = proj[:, 4 * hw:]


def _inproj(x2, ln_g, ln_b, w_in_bf, lb_logits, *, tm):
    n, d = x2.shape
    cols = w_in_bf.shape[1]
    hw = lb_logits.shape[1]
    uw = cols - 4 * hw
    row = lambda w: pl.BlockSpec((tm, w), lambda i: (i, 0))
    full = lambda a: pl.BlockSpec(a.shape, lambda i: (0,) * a.ndim)
    outs = [jax.ShapeDtypeStruct((n, d), F32)] + \
           [jax.ShapeDtypeStruct((n, hw), F32)] * 5 + [jax.ShapeDtypeStruct((n, uw), F32)]
    return pl.pallas_call(
        functools.partial(_inproj_kernel, hw=hw),
        grid=(n // tm,),
        in_specs=[row(d), full(ln_g), full(ln_b), full(w_in_bf), full(lb_logits)],
        out_specs=[row(d)] + [row(hw)] * 5 + [row(uw)],
        out_shape=outs,
        compiler_params=_params("parallel"),
        name="inproj",
    )(x2, ln_g, ln_b, w_in_bf, lb_logits)


def _hgrn_masks():
    t = np.arange(CHUNK)
    le = t[None, :] <= t[:, None]
    same_sub = (t[None, :] // SUB) == (t[:, None] // SUB)
    m = np.concatenate([le, le & same_sub, same_sub], axis=0)
    return m.astype(np.float32)


def _diag_selector():
    s = np.repeat(np.arange(SUB), HG_DK)
    c = np.arange(CHUNK)
    return (s[:, None] == (c[None, :] % SUB)).astype(np.float32)


def _hgrn_kernel(q_ref, g_ref, k_ref, v_ref, gate_ref, ng_ref, m3_ref, sel_ref,
                 o_ref, state_ref, *, tc):
    nch = tc // CHUNK
    nsub = tc // SUB

    @pl.when(pl.program_id(1) == 0)
    def _():
        state_ref[...] = jnp.zeros_like(state_ref)

    g = g_ref[...]
    g_hi = _bf(g)
    r1 = g - g_hi.astype(F32)
    g_mid = _bf(r1)
    g_lo = _bf(r1 - g_mid.astype(F32))
    g_parts = jnp.concatenate([g_hi, g_mid, g_lo], axis=1)
    hw = g.shape[1]
    m3 = m3_ref[...]
    chunk_sums = []
    for c in range(nch):
        r = jnp.dot(m3, g_parts[c * CHUNK:(c + 1) * CHUNK], preferred_element_type=F32)
        chunk_sums.append(r[:, 0:hw] + r[:, hw:2 * hw] + r[:, 2 * hw:])
    part = lambda i: jnp.concatenate(
        [cs_[i * CHUNK:(i + 1) * CHUNK] for cs_ in chunk_sums], axis=0)
    cum_all, cs_all, tot_all = part(0), part(1), part(2)
    row = lax.broadcasted_iota(jnp.int32, (CHUNK, CHUNK), 0)
    col = lax.broadcasted_iota(jnp.int32, (CHUNK, CHUNK), 1)
    blk = row // SUB - col // SUB
    diag_ok = (blk == 0) & (col <= row)
    sel = sel_ref[...]
    ng = ng_ref[...]

    for h in range(HG_HEADS):
        lanes = slice(h * HG_DK, (h + 1) * HG_DK)
        q = q_ref[:, lanes]
        k = k_ref[:, lanes]
        v = v_ref[:, lanes]
        cum = cum_all[:, lanes]
        cs = cs_all[:, lanes]
        tot = tot_all[:, lanes]
        cum3 = cum.reshape(nch, CHUNK, HG_DK)
        last = cum3[:, CHUNK - 1:CHUNK, :]
        q_inter = _bf(q * jnp.exp(cum))
        k_state = _bf((k.reshape(nch, CHUNK, HG_DK) * jnp.exp(last - cum3)).reshape(tc, HG_DK))
        v_bf = _bf(v)
        q_sub = q * jnp.exp(cs)
        k_sub = _bf(k * jnp.exp(tot - cs))
        zeros1 = jnp.zeros((SUB, HG_DK), F32)
        tot_p1 = jnp.concatenate([zeros1, tot[:tc - SUB]], axis=0)
        tot_p2 = jnp.concatenate([zeros1, zeros1, tot[:tc - 2 * SUB]], axis=0)
        q_sub2 = q_sub * jnp.exp(tot_p1)
        q_sub3 = q_sub2 * jnp.exp(tot_p2)
        cs3 = (cs * LOG2E).reshape(nsub, SUB, HG_DK)
        q3 = _bf(q).reshape(nsub, SUB, HG_DK)
        k3 = _bf(k).reshape(nsub, SUB, HG_DK)
        pieces = []
        for s in range(SUB):
            decay = jnp.exp2(jnp.minimum(cs3 - cs3[:, s:s + 1, :], 0.0))
            pieces.append((q3 * _bf(decay) * k3[:, s:s + 1, :]).reshape(tc, HG_DK))
        diag = jnp.dot(jnp.concatenate(pieces, axis=1), sel,
                       preferred_element_type=F32)

        st = state_ref[h]
        outs = []
        for c in range(nch):
            rows = slice(c * CHUNK, (c + 1) * CHUNK)
            inter = lax.dot_general(q_inter[rows], _bf(st), (((1,), (1,)), ((), ())),
                                    preferred_element_type=F32)
            lhs = _bf(jnp.concatenate([q_sub[rows], q_sub2[rows], q_sub3[rows]], axis=0))
            off = lax.dot_general(lhs, k_sub[rows], (((1,), (1,)), ((), ())),
                                  preferred_element_type=F32)
            sc = jnp.where(diag_ok, diag[rows],
                           jnp.where(blk == 1, off[0:CHUNK],
                                     jnp.where(blk == 2, off[CHUNK:2 * CHUNK],
                                               jnp.where(blk == 3, off[2 * CHUNK:], 0.0))))
            intra = jnp.dot(_bf(sc), v_bf[rows], preferred_element_type=F32)
            outs.append(inter + intra)
            upd = lax.dot_general(v_bf[rows], k_state[rows], (((0,), (0,)), ((), ())),
                                  preferred_element_type=F32)
            st = st * jnp.exp(last[c]) + upd
        state_ref[h] = st
        o = jnp.concatenate(outs, axis=0)
        rms = lax.rsqrt(jnp.mean(o * o, axis=-1, keepdims=True) + RMS_EPS)
        gate = gate_ref[:, lanes]
        o_ref[:, lanes] = o * rms * ng * (gate * _sigmoid(gate))


def _hgrn(q, gl, k, v, gate, norm_g, *, batch, tc):
    n, hw = q.shape
    nblk = n // batch // tc
    m3 = jnp.asarray(_hgrn_masks(), dtype=BF16)
    sel = jnp.asarray(_diag_selector(), dtype=BF16)
    row = pl.BlockSpec((tc, hw), lambda b, s: (b * nblk + s, 0))
    full = lambda a: pl.BlockSpec(a.shape, lambda b, s: (0,) * a.ndim)
    return pl.pallas_call(
        functools.partial(_hgrn_kernel, tc=tc),
        grid=(batch, nblk),
        in_specs=[row] * 5 + [full(norm_g), full(m3), full(sel)],
        out_specs=row,
        out_shape=jax.ShapeDtypeStruct((n, hw), F32),
        scratch_shapes=[pltpu.VMEM((HG_HEADS, HG_DK, HG_DK), F32)],
        compiler_params=_params("parallel", "arbitrary"),
        name="hgrn2",
    )(q, gl, k, v, gate, norm_g, m3, sel)


def _time_major_perm(batch, tt):
    r = np.arange(batch * tt)
    p = np.zeros((batch * tt, batch * tt), np.float32)
    p[r, (r % batch) * tt + r // batch] = 1.0
    return p


def _s5_kernel(u_ref, perm_ref, permt_ref, ar_ref, ai_ref, wb_ref, wc_ref, d_ref, wg_ref, bg_ref,
               o_ref, bu_ref, st_ref, *, tt):
    batch, _, width = u_ref.shape
    ns = ar_ref.shape[1]
    half = ns // 2

    @pl.when(pl.program_id(0) == 0)
    def _():
        st_ref[...] = jnp.zeros_like(st_ref)

    u2 = u_ref[...].reshape(batch * tt, width)
    u_hi = _bf(u2)
    u_lo = _bf(u2 - u_hi.astype(F32))
    perm = perm_ref[...]
    u_hi_tm = jnp.dot(perm, u_hi, preferred_element_type=F32)
    u_tm = u_hi_tm + jnp.dot(perm, u_lo, preferred_element_type=F32)
    bu_ref[...] = jnp.dot(_bf(u_hi_tm), wb_ref[...], preferred_element_type=F32)

    for part in range(2):
        lo = part * half
        ar = jnp.broadcast_to(ar_ref[:, lo:lo + half], (batch, half))
        ai = jnp.broadcast_to(ai_ref[:, lo:lo + half], (batch, half))

        def step(t, carry, lo=lo, ar=ar, ai=ai):
            xr, xi = carry
            rows = pl.ds(pl.multiple_of(t * batch, batch), batch)
            nr = ar * xr - ai * xi + bu_ref[rows, lo:lo + half]
            ni = ar * xi + ai * xr + bu_ref[rows, ns + lo:ns + lo + half]
            bu_ref[rows, lo:lo + half] = nr
            bu_ref[rows, ns + lo:ns + lo + half] = ni
            return nr, ni

        xr, xi = lax.fori_loop(0, tt, step, (st_ref[:, lo:lo + half],
                                             st_ref[:, ns + lo:ns + lo + half]))
        st_ref[:, lo:lo + half] = xr
        st_ref[:, ns + lo:ns + lo + half] = xi

    y = jnp.dot(_bf(bu_ref[...]), wc_ref[...], preferred_element_type=F32) + d_ref[...] * u_tm
    z = _gelu(y)
    gl = jnp.dot(_bf(z), wg_ref[...], preferred_element_type=F32) + bg_ref[...]
    out_tm = _bf(z * _sigmoid(gl))
    out = jnp.dot(permt_ref[...], out_tm, preferred_element_type=F32)
    o_ref[...] = _bf(out).reshape(batch, tt, width)


def _s5(u3, abar_re, abar_im, wb, wc, d, wg, bg, *, tt):
    batch, seq, width = u3.shape
    ns = abar_re.shape[1]
    perm = _time_major_perm(batch, tt)
    perm_bf = jnp.asarray(perm, dtype=BF16)
    permt_bf = jnp.asarray(perm.T, dtype=BF16)
    blk = pl.BlockSpec((batch, tt, width), lambda s: (0, s, 0))
    full = lambda a: pl.BlockSpec(a.shape, lambda s: (0,) * a.ndim)
    consts = (perm_bf, permt_bf, abar_re, abar_im, wb, wc, d, wg, bg)
    return pl.pallas_call(
        functools.partial(_s5_kernel, tt=tt),
        grid=(seq // tt,),
        in_specs=[blk] + [full(a) for a in consts],
        out_specs=blk,
        out_shape=jax.ShapeDtypeStruct(u3.shape, BF16),
        scratch_shapes=[pltpu.VMEM((batch * tt, 2 * ns), F32),
                        pltpu.VMEM((batch, 2 * ns), F32)],
        compiler_params=_params("arbitrary"),
        name="s5",
    )(u3, *consts)


def _s5_discretise(a_re, a_im, log_step, b_re, b_im, c_re, c_im):
    groups, nstate, p = b_re.shape
    step = jnp.exp(log_step.astype(F32))[:, None]
    mag = jnp.exp(step * a_re)
    abar_re = mag * jnp.cos(step * a_im)
    abar_im = mag * jnp.sin(step * a_im)
    num_re, num_im = abar_re - 1.0, abar_im
    den = jnp.square(a_re) + jnp.square(a_im)
    coef_re = ((num_re * a_re + num_im * a_im) / den)[..., None]
    coef_im = ((num_im * a_re - num_re * a_im) / den)[..., None]
    bbar_re = coef_re * b_re - coef_im * b_im
    bbar_im = coef_re * b_im + coef_im * b_re
    eye = jnp.eye(groups, dtype=F32)
    bd = lambda m: jnp.einsum("gnp,gh->gphn", m, eye).reshape(groups * p, groups * nstate)
    wb = jnp.concatenate([bd(bbar_re), bd(bbar_im)], axis=1)
    cd = lambda m: jnp.einsum("gpn,gh->gnhp", m, eye).reshape(groups * nstate, groups * p)
    wc = jnp.concatenate([cd(c_re), -cd(c_im)], axis=0)
    flat = lambda m: m.reshape(1, groups * nstate)
    return flat(abar_re), flat(abar_im), _bf(wb), _bf(wc)


def _extract_top(s, want_rank):
    vals = []
    rank = jnp.full(s.shape, float(PEER_TOPK), F32)
    cur = s
    for r in range(PEER_TOPK):
        m = jnp.max(cur, axis=0, keepdims=True)
        vals.append(m)
        hit = cur == m
        if want_rank:
            rank = jnp.where(hit, float(r), rank)
        cur = jnp.where(hit, NEG_INF, cur)
    return vals, rank


def _stack_rows(rows, nrows):
    t = rows[0].shape[1]
    idx = lax.broadcasted_iota(jnp.int32, (nrows, t), 0)
    out = jnp.zeros((nrows, t), F32)
    for r, v in enumerate(rows):
        out = jnp.where(idx == r, v, out)
    return out


def _route_head(s1, s2):
    v1, _ = _extract_top(s1, False)
    v2, rank2 = _extract_top(s2, True)
    t = s1.shape[1]
    v2_lo = _stack_rows(v2[:8], 8)
    v2_hi = _stack_rows(v2[8:], 8)
    v1_hi = _stack_rows(v1[8:], 8)
    idx8 = lax.broadcasted_iota(jnp.int32, (8, t), 0)
    cand = [v1[0] + v2_lo, v1[0] + v2_hi]
    for a in range(1, 8):
        nb = PEER_TOPK // (a + 1)
        c = v1[a] + v2_lo
        cand.append(c if nb >= 8 else jnp.where(idx8 < nb, c, NEG_INF))
    cand.append(v1_hi + v2[0])
    cur = jnp.concatenate(cand, axis=0)
    top = v1[0] + v2[0]
    z = jnp.zeros_like(top)
    tau = top
    for _ in range(PEER_TOPK):
        tau = jnp.max(cur, axis=0, keepdims=True)
        z = z + jnp.exp(tau - top)
        cur = jnp.where(cur == tau, NEG_INF, cur)
    cnt = jnp.zeros(s1.shape, F32)
    for b in range(PEER_TOPK):
        cnt = cnt + jnp.where(s1 + v2[b] >= tau, 1.0, 0.0)
    e1 = jnp.exp(s1 - v1[0])
    e2 = jnp.exp(s2 - v2[0]) / z
    return e1, cnt, e2, rank2


def _route_kernel(ma_ref, mb_ref, h0_ref, wo_ref, g_ref, b_ref, wq_ref, k1_ref, k2_ref,
                  h1t_ref, h1tb_ref, e1_ref, cn_ref, e2_ref, r2_ref):
    hw = ma_ref.shape[1]
    mixed = jnp.dot(_bf(ma_ref[...]), wo_ref[0:hw, :], preferred_element_type=F32)
    mixed = mixed + jnp.dot(_bf(mb_ref[...]), wo_ref[hw:, :], preferred_element_type=F32)
    h1 = _layer_norm(DN_ALPHA * h0_ref[...] + mixed, g_ref[...], b_ref[...])
    h1t = h1.T
    h1t_ref[...] = h1t
    h1tb = _bf(h1t)
    h1tb_ref[...] = h1tb
    qt = jnp.dot(wq_ref[...], h1tb, preferred_element_type=F32)
    k1 = k1_ref[...]
    k2 = k2_ref[...]
    for h in range(PEER_HEADS):
        base = h * 2 * PEER_HALF
        s1 = jnp.dot(k1, _bf(qt[base:base + PEER_HALF]), preferred_element_type=F32)
        s2 = jnp.dot(k2, _bf(qt[base + PEER_HALF:base + 2 * PEER_HALF]),
                     preferred_element_type=F32)
        e1, cnt, e2, rank2 = _route_head(s1, s2)
        e1_ref[h] = e1
        cn_ref[h] = cnt
        e2_ref[h] = _bf(e2)
        r2_ref[h] = _bf(rank2)


def _route(mix_a, mix_b, h0, w_out_bf, ln_g, ln_b, wq_t_bf, k1_bf, k2_bf, *, tm):
    n, d = h0.shape
    hw = mix_a.shape[1]
    row = lambda w: pl.BlockSpec((tm, w), lambda i: (i, 0))
    full = lambda a: pl.BlockSpec(a.shape, lambda i: (0,) * a.ndim)
    colblk = pl.BlockSpec((d, tm), lambda i: (0, i))
    tab = pl.BlockSpec((PEER_HEADS, PEER_KEYS, tm), lambda i: (0, 0, i))
    tab_shape = (PEER_HEADS, PEER_KEYS, n)
    return pl.pallas_call(
        _route_kernel,
        grid=(n // tm,),
        in_specs=[row(hw), row(hw), row(d)] +
                 [full(a) for a in (w_out_bf, ln_g, ln_b, wq_t_bf, k1_bf, k2_bf)],
        out_specs=[colblk, colblk, tab, tab, tab, tab],
        out_shape=[jax.ShapeDtypeStruct((d, n), F32), jax.ShapeDtypeStruct((d, n), BF16),
                   jax.ShapeDtypeStruct(tab_shape, F32), jax.ShapeDtypeStruct(tab_shape, F32),
                   jax.ShapeDtypeStruct(tab_shape, BF16), jax.ShapeDtypeStruct(tab_shape, BF16)],
        compiler_params=_params("parallel"),
        name="route",
    )(mix_a, mix_b, h0, w_out_bf, ln_g, ln_b, wq_t_bf, k1_bf, k2_bf)


def _experts_kernel(wd_ref, wut_ref, h1tb_ref, h1t_ref, e1_ref, cn_ref, e2_ref, r2_ref,
                    g_ref, b_ref, o_ref, acc_ref, p_ref, *, ib):
    kstep = pl.program_id(1)

    @pl.when(kstep == 0)
    def _():
        acc_ref[...] = jnp.zeros_like(acc_ref)

    a = jnp.dot(wd_ref[...], h1tb_ref[...], preferred_element_type=F32)
    tt = a.shape[1]
    for ii in range(ib):
        gsum = jnp.zeros((PEER_KEYS, tt), BF16)
        for h in range(PEER_HEADS):
            e1 = _bf(jnp.broadcast_to(e1_ref[h, ii:ii + 1, :], (PEER_KEYS, tt)))
            cn = _bf(jnp.broadcast_to(cn_ref[h, ii:ii + 1, :], (PEER_KEYS, tt)))
            val = e2_ref[h] * e1
            gsum = gsum + jnp.where(r2_ref[h] < cn, val, jnp.zeros_like(val))
        rows = slice(ii * PEER_KEYS, (ii + 1) * PEER_KEYS)
        p_ref[rows, :] = _gelu(_bf(a[rows])) * gsum
    acc_ref[...] += jnp.dot(wut_ref[...], p_ref[...], preferred_element_type=F32)

    @pl.when(kstep == pl.num_programs(1) - 1)
    def _():
        y = DN_ALPHA * h1t_ref[...] + acc_ref[...]
        mu = jnp.mean(y, axis=0, keepdims=True)
        yc = y - mu
        var = jnp.mean(yc * yc, axis=0, keepdims=True)
        yn = yc * lax.rsqrt(var + LN_EPS)
        o_ref[...] = yn.T * g_ref[...] + b_ref[...]


def _experts(wd_bf, wut_bf, h1tb, h1t, e1, cn, e2, r2, ln_g, ln_b, *, tt, ib):
    d, n = h1t.shape
    nexp = wd_bf.shape[0]
    eb = ib * PEER_KEYS
    tokcol = pl.BlockSpec((d, tt), lambda j, k: (0, j))
    tab_i = pl.BlockSpec((PEER_HEADS, ib, tt), lambda j, k: (0, k, j))
    tab_full = pl.BlockSpec((PEER_HEADS, PEER_KEYS, tt), lambda j, k: (0, 0, j))
    full = lambda a: pl.BlockSpec(a.shape, lambda j, k: (0,) * a.ndim)
    return pl.pallas_call(
        functools.partial(_experts_kernel, ib=ib),
        grid=(n // tt, nexp // eb),
        in_specs=[pl.BlockSpec((eb, d), lambda j, k: (k, 0)),
                  pl.BlockSpec((d, eb), lambda j, k: (0, k)),
                  tokcol, tokcol, tab_i, tab_i, tab_full, tab_full, full(ln_g), full(ln_b)],
        out_specs=pl.BlockSpec((tt, d), lambda j, k: (j, 0)),
        out_shape=jax.ShapeDtypeStruct((n, d), F32),
        scratch_shapes=[pltpu.VMEM((d, tt), F32), pltpu.VMEM((eb, tt), BF16)],
        compiler_params=_params("parallel", "arbitrary"),
        name="experts",
    )(wd_bf, wut_bf, h1tb, h1t, e1, cn, e2, r2, ln_g, ln_b)


def _tile(n, pref):
    return pref if n % pref == 0 else n


def kernel(x, ln0_g, ln0_b, w_in, hg_lb_logits, hg_norm_g, s5_a_re, s5_a_im, s5_log_step,
           s5_b_re, s5_b_im, s5_c_re, s5_c_im, s5_d, w_glu, b_glu, w_out, ln1_g, ln1_b,
           w_query, peer_keys_1, peer_keys_2, peer_down, peer_up, ln2_g, ln2_b):
    assert w_in.shape[0] == 1, "single-layer trunk"
    batch, seq, d = x.shape
    n = batch * seq
    r2d = lambda a: a.reshape(1, -1).astype(F32)

    x2 = x.reshape(n, d).astype(F32)
    h0, q, gl, k, v, gate, u = _inproj(
        x2, r2d(ln0_g), r2d(ln0_b), _bf(w_in[0]), hg_lb_logits.astype(F32),
        tm=_tile(n, 512))

    mix_a = _hgrn(q, gl, k, v, gate, r2d(hg_norm_g[0]), batch=batch, tc=_tile(seq, 512))

    abar_re, abar_im, wb, wc = _s5_discretise(
        s5_a_re[0].astype(F32), s5_a_im[0].astype(F32), s5_log_step[0],
        s5_b_re[0].astype(F32), s5_b_im[0].astype(F32), s5_c_re[0], s5_c_im[0])
    mix_b = _s5(u.reshape(batch, seq, -1), abar_re, abar_im, wb, wc, r2d(s5_d[0]),
                _bf(w_glu[0]), r2d(b_glu[0]), tt=_tile(seq, 64)).reshape(n, -1)

    h1t, h1tb, e1, cn, e2, r2 = _route(
        mix_a, mix_b, h0, _bf(w_out[0]), r2d(ln1_g[0]), r2d(ln1_b[0]),
        _bf(w_query[0].T), _bf(peer_keys_1[0]), _bf(peer_keys_2[0]), tm=_tile(n, 256))

    out = _experts(_bf(peer_down[0]), _bf(peer_up[0].T), h1tb, h1t, e1, cn, e2, r2,
                   r2d(ln2_g[0]), r2d(ln2_b[0]), tt=_tile(n, 512), ib=16)
    return out.reshape(batch, seq, d).astype(x.dtype)
```

```python
import functools
import math

import jax
import jax.numpy as jnp
import numpy as np
from jax import lax
from jax.experimental import pallas as pl
from jax.experimental.pallas import tpu as pltpu

F32 = jnp.float32
BF16 = jnp.bfloat16

CHUNK = 64
SUB = 16
HG_DK = 128
HG_HEADS = 4
S5_GROUP = 16
S5_STATE = 64
PEER_KEYS = 128
PEER_HEADS = 8
PEER_TOPK = 16
PEER_HALF = 128
LN_EPS = 1e-5
RMS_EPS = 1e-6
DN_ALPHA = 2.0 ** 0.25
NEG_INF = float("-inf")
LANES = 128
LOG2E = math.log2(math.e)

VMEM_LIMIT_BYTES = 56 * 1024 * 1024


def _params(*semantics, flags=None):
    return pltpu.CompilerParams(dimension_semantics=semantics,
                                vmem_limit_bytes=VMEM_LIMIT_BYTES, flags=flags)


def _bf(x):
    return x.astype(BF16)


def _layer_norm(x, g, b):
    mu = jnp.mean(x, axis=-1, keepdims=True)
    xc = x - mu
    var = jnp.mean(xc * xc, axis=-1, keepdims=True)
    return xc * lax.rsqrt(var + LN_EPS) * g + b


def _sigmoid(x):
    return 1.0 / (1.0 + jnp.exp(-x))


def _gelu(x):
    k0 = -2.0 * math.sqrt(2.0 / math.pi) * math.log2(math.e)
    w = x * (x * x * (k0 * 0.044715) + k0)
    return x / (1.0 + jnp.exp2(w))


def _inproj_kernel(x_ref, g_ref, b_ref, w_ref, lbl_ref,
                   h_ref, q_ref, gl_ref, k_ref, v_ref, gate_ref, u_ref, *, hw):
    h = _layer_norm(x_ref[...], g_ref[...], b_ref[...])
    h_ref[...] = h
    proj = jnp.dot(_bf(h), w_ref[...], preferred_element_type=F32)
    logits = lbl_ref[...]
    e = jnp.exp(logits - jnp.max(logits, axis=0, keepdims=True))
    lb = e[0:1, :] / jnp.sum(e, axis=0, keepdims=True)
    forget = lb + (1.0 - lb) * _sigmoid(proj[:, hw:2 * hw])
    q_ref[...] = proj[:, 0:hw] * (HG_DK ** -0.5)
    gl_ref[...] = jnp.log(forget)
    k_ref[...] = 1.0 - forget
    v_ref[...] = proj[:, 2 * hw:3 * hw]
    gate_ref[...] = proj[:, 3 * hw:4 * hw]
    u_ref[...] = proj[:, 4 * hw:]


def _inproj(x2, ln_g, ln_b, w_in_bf, lb_logits, *, tm):
    n, d = x2.shape
    cols = w_in_bf.shape[1]
    hw = lb_logits.shape[1]
    uw = cols - 4 * hw
    row = lambda w: pl.BlockSpec((tm, w), lambda i: (i, 0))
    full = lambda a: pl.BlockSpec(a.shape, lambda i: (0,) * a.ndim)
    outs = [jax.ShapeDtypeStruct((n, d), F32)] + \
           [jax.ShapeDtypeStruct((n, hw), F32)] * 5 + [jax.ShapeDtypeStruct((n, uw), F32)]
    return pl.pallas_call(
        functools.partial(_inproj_kernel, hw=hw),
        grid=(n // tm,),
        in_specs=[row(d), full(ln_g), full(ln_b), full(w_in_bf), full(lb_logits)],
        out_specs=[row(d)] + [row(hw)] * 5 + [row(uw)],
        out_shape=outs,
        compiler_params=_params("parallel"),
        name="inproj",
    )(x2, ln_g, ln_b, w_in_bf, lb_logits)


def _hgrn_masks():
    t = np.arange(CHUNK)
    le = t[None, :] <= t[:, None]
    same_sub = (t[None, :] // SUB) == (t[:, None] // SUB)
    m = np.concatenate([le, le & same_sub, same_sub], axis=0)
    return m.astype(np.float32)


def _diag_selector():
    s = np.repeat(np.arange(SUB), HG_DK)
    c = np.arange(CHUNK)
    return (s[:, None] == (c[None, :] % SUB)).astype(np.float32)


def _hgrn_kernel(q_ref, g_ref, k_ref, v_ref, gate_ref, ng_ref, m3_ref, sel_ref,
                 o_ref, state_ref, *, tc):
    nch = tc // CHUNK
    nsub = tc // SUB

    @pl.when(pl.program_id(1) == 0)
    def _():
        state_ref[...] = jnp.zeros_like(state_ref)

    g = g_ref[...]
    g_hi = _bf(g)
    r1 = g - g_hi.astype(F32)
    g_mid = _bf(r1)
    g_lo = _bf(r1 - g_mid.astype(F32))
    g_parts = jnp.concatenate([g_hi, g_mid, g_lo], axis=1)
    hw = g.shape[1]
    m3 = m3_ref[...]
    chunk_sums = []
    for c in range(nch):
        r = jnp.dot(m3, g_parts[c * CHUNK:(c + 1) * CHUNK], preferred_element_type=F32)
        chunk_sums.append(r[:, 0:hw] + r[:, hw:2 * hw] + r[:, 2 * hw:])
    part = lambda i: jnp.concatenate(
        [cs_[i * CHUNK:(i + 1) * CHUNK] for cs_ in chunk_sums], axis=0)
    cum_all, cs_all, tot_all = part(0), part(1), part(2)
    row = lax.broadcasted_iota(jnp.int32, (CHUNK, CHUNK), 0)
    col = lax.broadcasted_iota(jnp.int32, (CHUNK, CHUNK), 1)
    blk = row // SUB - col // SUB
    diag_ok = (blk == 0) & (col <= row)
    sel = sel_ref[...]
    ng = ng_ref[...]

    for h in range(HG_HEADS):
        lanes = slice(h * HG_DK, (h + 1) * HG_DK)
        q = q_ref[:, lanes]
        k = k_ref[:, lanes]
        v = v_ref[:, lanes]
        cum = cum_all[:, lanes]
        cs = cs_all[:, lanes]
        tot = tot_all[:, lanes]
        cum3 = cum.reshape(nch, CHUNK, HG_DK)
        last = cum3[:, CHUNK - 1:CHUNK, :]
        q_inter = _bf(q * jnp.exp(cum))
        k_state = _bf((k.reshape(nch, CHUNK, HG_DK) * jnp.exp(last - cum3)).reshape(tc, HG_DK))
        v_bf = _bf(v)
        q_sub = q * jnp.exp(cs)
        k_sub = _bf(k * jnp.exp(tot - cs))
        zeros1 = jnp.zeros((SUB, HG_DK), F32)
        tot_p1 = jnp.concatenate([zeros1, tot[:tc - SUB]], axis=0)
        tot_p2 = jnp.concatenate([zeros1, zeros1, tot[:tc - 2 * SUB]], axis=0)
        q_sub2 = q_sub * jnp.exp(tot_p1)
        q_sub3 = q_sub2 * jnp.exp(tot_p2)
        cs3 = (cs * LOG2E).reshape(nsub, SUB, HG_DK)
        q3 = _bf(q).reshape(nsub, SUB, HG_DK)
        k3 = _bf(k).reshape(nsub, SUB, HG_DK)
        pieces = []
        for s in range(SUB):
            decay = jnp.exp2(jnp.minimum(cs3 - cs3[:, s:s + 1, :], 0.0))
            pieces.append((q3 * _bf(decay) * k3[:, s:s + 1, :]).reshape(tc, HG_DK))
        diag = jnp.dot(jnp.concatenate(pieces, axis=1), sel,
                       preferred_element_type=F32)

        st = state_ref[h]
        outs = []
        for c in range(nch):
            rows = slice(c * CHUNK, (c + 1) * CHUNK)
            inter = lax.dot_general(q_inter[rows], _bf(st), (((1,), (1,)), ((), ())),
                                    preferred_element_type=F32)
            lhs = _bf(jnp.concatenate([q_sub[rows], q_sub2[rows], q_sub3[rows]], axis=0))
            off = lax.dot_general(lhs, k_sub[rows], (((1,), (1,)), ((), ())),
                                  preferred_element_type=F32)
            sc = jnp.where(diag_ok, diag[rows],
                           jnp.where(blk == 1, off[0:CHUNK],
                                     jnp.where(blk == 2, off[CHUNK:2 * CHUNK],
                                               jnp.where(blk == 3, off[2 * CHUNK:], 0.0))))
            intra = jnp.dot(_bf(sc), v_bf[rows], preferred_element_type=F32)
            outs.append(inter + intra)
            upd = lax.dot_general(v_bf[rows], k_state[rows], (((0,), (0,)), ((), ())),
                                  preferred_element_type=F32)
            st = st * jnp.exp(last[c]) + upd
        state_ref[h] = st
        o = jnp.concatenate(outs, axis=0)
        rms = lax.rsqrt(jnp.mean(o * o, axis=-1, keepdims=True) + RMS_EPS)
        gate = gate_ref[:, lanes]
        o_ref[:, lanes] = o * rms * ng * (gate * _sigmoid(gate))


def _hgrn(q, gl, k, v, gate, norm_g, *, batch, tc):
    n, hw = q.shape
    nblk = n // batch // tc
    m3 = jnp.asarray(_hgrn_masks(), dtype=BF16)
    sel = jnp.asarray(_diag_selector(), dtype=BF16)
    row = pl.BlockSpec((tc, hw), lambda b, s: (b * nblk + s, 0))
    full = lambda a: pl.BlockSpec(a.shape, lambda b, s: (0,) * a.ndim)
    return pl.pallas_call(
        functools.partial(_hgrn_kernel, tc=tc),
        grid=(batch, nblk),
        in_specs=[row] * 5 + [full(norm_g), full(m3), full(sel)],
        out_specs=row,
        out_shape=jax.ShapeDtypeStruct((n, hw), F32),
        scratch_shapes=[pltpu.VMEM((HG_HEADS, HG_DK, HG_DK), F32)],
        compiler_params=_params("parallel", "arbitrary"),
        name="hgrn2",
    )(q, gl, k, v, gate, norm_g, m3, sel)


def _time_major_perm(batch, tt):
    r = np.arange(batch * tt)
    p = np.zeros((batch * tt, batch * tt), np.float32)
    p[r, (r % batch) * tt + r // batch] = 1.0
    return p


def _s5_kernel(u_ref, perm_ref, permt_ref, ar_ref, ai_ref, wb_ref, wc_ref, d_ref, wg_ref, bg_ref,
               o_ref, bu_ref, st_ref, *, tt):
    batch, _, width = u_ref.shape
    ns = ar_ref.shape[1]
    half = ns // 2

    @pl.when(pl.program_id(0) == 0)
    def _():
        st_ref[...] = jnp.zeros_like(st_ref)

    u2 = u_ref[...].reshape(batch * tt, width)
    u_hi = _bf(u2)
    u_lo = _bf(u2 - u_hi.astype(F32))
    perm = perm_ref[...]
    u_hi_tm = jnp.dot(perm, u_hi, preferred_element_type=F32)
    u_tm = u_hi_tm + jnp.dot(perm, u_lo, preferred_element_type=F32)
    bu_ref[...] = jnp.dot(_bf(u_hi_tm), wb_ref[...], preferred_element_type=F32)

    for part in range(2):
        lo = part * half
        ar = jnp.broadcast_to(ar_ref[:, lo:lo + half], (batch, half))
        ai = jnp.broadcast_to(ai_ref[:, lo:lo + half], (batch, half))

        def step(t, carry, lo=lo, ar=ar, ai=ai):
            xr, xi = carry
            rows = pl.ds(pl.multiple_of(t * batch, batch), batch)
            nr = ar * xr - ai * xi + bu_ref[rows, lo:lo + half]
            ni = ar * xi + ai * xr + bu_ref[rows, ns + lo:ns + lo + half]
            bu_ref[rows, lo:lo + half] = nr
            bu_ref[rows, ns + lo:ns + lo + half] = ni
            return nr, ni

        xr, xi = lax.fori_loop(0, tt, step, (st_ref[:, lo:lo + half],
                                             st_ref[:, ns + lo:ns + lo + half]))
        st_ref[:, lo:lo + half] = xr
        st_ref[:, ns + lo:ns + lo + half] = xi

    y = jnp.dot(_bf(bu_ref[...]), wc_ref[...], preferred_element_type=F32) + d_ref[...] * u_tm
    z = _gelu(y)
    gl = jnp.dot(_bf(z), wg_ref[...], preferred_element_type=F32) + bg_ref[...]
    out_tm = _bf(z * _sigmoid(gl))
    out = jnp.dot(permt_ref[...], out_tm, preferred_element_type=F32)
    o_ref[...] = _bf(out).reshape(batch, tt, width)


def _s5(u3, abar_re, abar_im, wb, wc, d, wg, bg, *, tt):
    batch, seq, width = u3.shape
    ns = abar_re.shape[1]
    perm = _time_major_perm(batch, tt)
    perm_bf = jnp.asarray(perm, dtype=BF16)
    permt_bf = jnp.asarray(perm.T, dtype=BF16)
    blk = pl.BlockSpec((batch, tt, width), lambda s: (0, s, 0))
    full = lambda a: pl.BlockSpec(a.shape, lambda s: (0,) * a.ndim)
    consts = (perm_bf, permt_bf, abar_re, abar_im, wb, wc, d, wg, bg)
    return pl.pallas_call(
        functools.partial(_s5_kernel, tt=tt),
        grid=(seq // tt,),
        in_specs=[blk] + [full(a) for a in consts],
        out_specs=blk,
        out_shape=jax.ShapeDtypeStruct(u3.shape, BF16),
        scratch_shapes=[pltpu.VMEM((batch * tt, 2 * ns), F32),
                        pltpu.VMEM((batch, 2 * ns), F32)],
        compiler_params=_params("arbitrary"),
        name="s5",
    )(u3, *consts)


def _s5_discretise(a_re, a_im, log_step, b_re, b_im, c_re, c_im):
    groups, nstate, p = b_re.shape
    step = jnp.exp(log_step.astype(F32))[:, None]
    mag = jnp.exp(step * a_re)
    abar_re = mag * jnp.cos(step * a_im)
    abar_im = mag * jnp.sin(step * a_im)
    num_re, num_im = abar_re - 1.0, abar_im
    den = jnp.square(a_re) + jnp.square(a_im)
    coef_re = ((num_re * a_re + num_im * a_im) / den)[..., None]
    coef_im = ((num_im * a_re - num_re * a_im) / den)[..., None]
    bbar_re = coef_re * b_re - coef_im * b_im
    bbar_im = coef_re * b_im + coef_im * b_re
    eye = jnp.eye(groups, dtype=F32)
    bd = lambda m: jnp.einsum("gnp,gh->gphn", m, eye).reshape(groups * p, groups * nstate)
    wb = jnp.concatenate([bd(bbar_re), bd(bbar_im)], axis=1)
    cd = lambda m: jnp.einsum("gpn,gh->gnhp", m, eye).reshape(groups * nstate, groups * p)
    wc = jnp.concatenate([cd(c_re), -cd(c_im)], axis=0)
    flat = lambda m: m.reshape(1, groups * nstate)
    return flat(abar_re), flat(abar_im), _bf(wb), _bf(wc)


def _extract_top(s, want_rank):
    vals = []
    rank = jnp.full(s.shape, float(PEER_TOPK), F32)
    cur = s
    for r in range(PEER_TOPK):
        m = jnp.max(cur, axis=0, keepdims=True)
        vals.append(m)
        hit = cur == m
        if want_rank:
            rank = jnp.where(hit, float(r), rank)
        cur = jnp.where(hit, NEG_INF, cur)
    return vals, rank


def _stack_rows(rows, nrows):
    t = rows[0].shape[1]
    idx = lax.broadcasted_iota(jnp.int32, (nrows, t), 0)
    out = jnp.zeros((nrows, t), F32)
    for r, v in enumerate(rows):
        out = jnp.where(idx == r, v, out)
    return out


def _route_head(s1, s2):
    v1, _ = _extract_top(s1, False)
    v2, rank2 = _extract_top(s2, True)
    t = s1.shape[1]
    v2_lo = _stack_rows(v2[:8], 8)
    v2_hi = _stack_rows(v2[8:], 8)
    v1_hi = _stack_rows(v1[8:], 8)
    idx8 = lax.broadcasted_iota(jnp.int32, (8, t), 0)
    cand = [v1[0] + v2_lo, v1[0] + v2_hi]
    for a in range(1, 8):
        nb = PEER_TOPK // (a + 1)
        c = v1[a] + v2_lo
        cand.append(c if nb >= 8 else jnp.where(idx8 < nb, c, NEG_INF))
    cand.append(v1_hi + v2[0])
    cur = jnp.concatenate(cand, axis=0)
    top = v1[0] + v2[0]
    z = jnp.zeros_like(top)
    tau = top
    for _ in range(PEER_TOPK):
        tau = jnp.max(cur, axis=0, keepdims=True)
        z = z + jnp.exp(tau - top)
        cur = jnp.where(cur == tau, NEG_INF, cur)
    cnt = jnp.zeros(s1.shape, F32)
    for b in range(PEER_TOPK):
        cnt = cnt + jnp.where(s1 + v2[b] >= tau, 1.0, 0.0)
    e1 = jnp.exp(s1 - v1[0])
    e2 = jnp.exp(s2 - v2[0]) / z
    return e1, cnt, e2, rank2


def _route_kernel(ma_ref, mb_ref, h0_ref, wo_ref, g_ref, b_ref, wq_ref, k1_ref, k2_ref,
                  h1t_ref, h1tb_ref, e1_ref, cn_ref, e2_ref, r2_ref):
    hw = ma_ref.shape[1]
    mixed = jnp.dot(_bf(ma_ref[...]), wo_ref[0:hw, :], preferred_element_type=F32)
    mixed = mixed + jnp.dot(_bf(mb_ref[...]), wo_ref[hw:, :], preferred_element_type=F32)
    h1 = _layer_norm(DN_ALPHA * h0_ref[...] + mixed, g_ref[...], b_ref[...])
    h1t = h1.T
    h1t_ref[...] = h1t
    h1tb = _bf(h1t)
    h1tb_ref[...] = h1tb
    qt = jnp.dot(wq_ref[...], h1tb, preferred_element_type=F32)
    k1 = k1_ref[...]
    k2 = k2_ref[...]
    for h in range(PEER_HEADS):
        base = h * 2 * PEER_HALF
        s1 = jnp.dot(k1, _bf(qt[base:base + PEER_HALF]), preferred_element_type=F32)
        s2 = jnp.dot(k2, _bf(qt[base + PEER_HALF:base + 2 * PEER_HALF]),
                     preferred_element_type=F32)
        e1, cnt, e2, rank2 = _route_head(s1, s2)
        e1_ref[h] = e1
        cn_ref[h] = cnt
        e2_ref[h] = _bf(e2)
        r2_ref[h] = _bf(rank2)


def _route(mix_a, mix_b, h0, w_out_bf, ln_g, ln_b, wq_t_bf, k1_bf, k2_bf, *, tm):
    n, d = h0.shape
    hw = mix_a.shape[1]
    row = lambda w: pl.BlockSpec((tm, w), lambda i: (i, 0))
    full = lambda a: pl.BlockSpec(a.shape, lambda i: (0,) * a.ndim)
    colblk = pl.BlockSpec((d, tm), lambda i: (0, i))
    tab = pl.BlockSpec((PEER_HEADS, PEER_KEYS, tm), lambda i: (0, 0, i))
    tab_shape = (PEER_HEADS, PEER_KEYS, n)
    return pl.pallas_call(
        _route_kernel,
        grid=(n // tm,),
        in_specs=[row(hw), row(hw), row(d)] +
                 [full(a) for a in (w_out_bf, ln_g, ln_b, wq_t_bf, k1_bf, k2_bf)],
        out_specs=[colblk, colblk, tab, tab, tab, tab],
        out_shape=[jax.ShapeDtypeStruct((d, n), F32), jax.ShapeDtypeStruct((d, n), BF16),
                   jax.ShapeDtypeStruct(tab_shape, F32), jax.ShapeDtypeStruct(tab_shape, F32),
                   jax.ShapeDtypeStruct(tab_shape, BF16), jax.ShapeDtypeStruct(tab_shape, BF16)],
        compiler_params=_params("parallel"),
        name="route",
    )(mix_a, mix_b, h0, w_out_bf, ln_g, ln_b, wq_t_bf, k1_bf, k2_bf)


def _dep_zero(x):
    u = pltpu.bitcast(x, jnp.uint32)
    u = lax.shift_right_logical(lax.shift_right_logical(u, jnp.uint32(16)), jnp.uint32(16))
    return pltpu.bitcast(u, BF16)


def _experts_kernel(wd0_ref, wd_ref, wut_ref, h1tb_ref, h1t_ref, e1_ref, cn_ref, e2_ref, r2_ref,
                    g_ref, b_ref, o_ref, acc_ref, a0_ref, a1_ref, p_ref, *, ib, nk):
    s = pl.program_id(1)
    d, tt = h1tb_ref.shape
    group = 2
    ngroup = ib // group
    kchunk = d // ngroup
    half = tt // 2

    @pl.when(s == 0)
    def _():
        acc_ref[...] = jnp.zeros_like(acc_ref)
        a0_ref[...] = jnp.dot(wd0_ref[...], h1tb_ref[...], preferred_element_type=F32)

    def step(a_cur, a_nxt):
        for hf in range(2):
            hl = pl.ds(hf * half, half)
            rhs = []
            for ig in range(ngroup):
                pc = None
                for u in range(group):
                    ii = ig * group + u
                    gsum = jnp.zeros((PEER_KEYS, half), BF16)
                    for h in range(PEER_HEADS):
                        e1 = _bf(jnp.broadcast_to(e1_ref[h, ii:ii + 1, hl], gsum.shape))
                        cn = _bf(jnp.broadcast_to(cn_ref[h, ii:ii + 1, hl], gsum.shape))
                        val = e2_ref[h, :, hl] * e1
                        gsum = gsum + jnp.where(r2_ref[h, :, hl] < cn, val, jnp.zeros_like(val))
                    rows = pl.ds(ii * PEER_KEYS, PEER_KEYS)
                    pc = _gelu(_bf(a_cur[rows, hl])) * gsum
                    p_ref[rows, hl] = pc
                zero = jnp.tile(_dep_zero(pc[0:16, 0:LANES]), (kchunk // 16, half // LANES))
                rhs.append(h1tb_ref[pl.ds(ig * kchunk, kchunk), hl] + zero)
            a_nxt[:, hl] = jnp.dot(wd_ref[...], jnp.concatenate(rhs, axis=0),
                                   preferred_element_type=F32)
            acc_ref[:, hl] += jnp.dot(wut_ref[...], p_ref[:, hl], preferred_element_type=F32)

    @pl.when(s % 2 == 0)
    def _():
        step(a0_ref, a1_ref)

    @pl.when(s % 2 == 1)
    def _():
        step(a1_ref, a0_ref)

    @pl.when(s == nk - 1)
    def _():
        y = DN_ALPHA * h1t_ref[...] + acc_ref[...]
        mu = jnp.mean(y, axis=0, keepdims=True)
        yc = y - mu
        var = jnp.mean(yc * yc, axis=0, keepdims=True)
        yn = yc * lax.rsqrt(var + LN_EPS)
        o_ref[...] = yn.T * g_ref[...] + b_ref[...]


def _experts(wd_bf, wut_bf, h1tb, h1t, e1, cn, e2, r2, ln_g, ln_b, *, tt, ib):
    d, n = h1t.shape
    nexp = wd_bf.shape[0]
    eb = ib * PEER_KEYS
    nk = nexp // eb
    last = nk - 1
    tokcol = pl.BlockSpec((d, tt), lambda j, s: (0, j))
    tab_i = pl.BlockSpec((PEER_HEADS, ib, tt), lambda j, s: (0, s, j))
    tab_full = pl.BlockSpec((PEER_HEADS, PEER_KEYS, tt), lambda j, s: (0, 0, j))
    full = lambda a: pl.BlockSpec(a.shape, lambda j, s: (0,) * a.ndim)
    return pl.pallas_call(
        functools.partial(_experts_kernel, ib=ib, nk=nk),
        grid=(n // tt, nk),
        in_specs=[pl.BlockSpec((eb, d), lambda j, s: (0, 0)),
                  pl.BlockSpec((eb, d), lambda j, s: (jnp.minimum(s + 1, last), 0)),
                  pl.BlockSpec((d, eb), lambda j, s: (0, s)),
                  tokcol, tokcol, tab_i, tab_i, tab_full, tab_full, full(ln_g), full(ln_b)],
        out_specs=pl.BlockSpec((tt, d), lambda j, s: (j, 0)),
        out_shape=jax.ShapeDtypeStruct((n, d), F32),
        scratch_shapes=[pltpu.VMEM((d, tt), F32),
                        pltpu.VMEM((eb, tt), F32), pltpu.VMEM((eb, tt), F32),
                        pltpu.VMEM((eb, tt), BF16)],
        compiler_params=_params("parallel", "arbitrary"),
        name="experts",
    )(wd_bf, wd_bf, wut_bf, h1tb, h1t, e1, cn, e2, r2, ln_g, ln_b)


def _tile(n, pref):
    return pref if n % pref == 0 else n


def kernel(x, ln0_g, ln0_b, w_in, hg_lb_logits, hg_norm_g, s5_a_re, s5_a_im, s5_log_step,
           s5_b_re, s5_b_im, s5_c_re, s5_c_im, s5_d, w_glu, b_glu, w_out, ln1_g, ln1_b,
           w_query, peer_keys_1, peer_keys_2, peer_down, peer_up, ln2_g, ln2_b):
    assert w_in.shape[0] == 1, "single-layer trunk"
    batch, seq, d = x.shape
    n = batch * seq
    r2d = lambda a: a.reshape(1, -1).astype(F32)

    x2 = x.reshape(n, d).astype(F32)
    h0, q, gl, k, v, gate, u = _inproj(
        x2, r2d(ln0_g), r2d(ln0_b), _bf(w_in[0]), hg_lb_logits.astype(F32),
        tm=_tile(n, 512))

    mix_a = _hgrn(q, gl, k, v, gate, r2d(hg_norm_g[0]), batch=batch, tc=_tile(seq, 512))

    abar_re, abar_im, wb, wc = _s5_discretise(
        s5_a_re[0].astype(F32), s5_a_im[0].astype(F32), s5_log_step[0],
        s5_b_re[0].astype(F32), s5_b_im[0].astype(F32), s5_c_re[0], s5_c_im[0])
    mix_b = _s5(u.reshape(batch, seq, -1), abar_re, abar_im, wb, wc, r2d(s5_d[0]),
                _bf(w_glu[0]), r2d(b_glu[0]), tt=_tile(seq, 64)).reshape(n, -1)

    h1t, h1tb, e1, cn, e2, r2 = _route(
        mix_a, mix_b, h0, _bf(w_out[0]), r2d(ln1_g[0]), r2d(ln1_b[0]),
        _bf(w_query[0].T), _bf(peer_keys_1[0]), _bf(peer_keys_2[0]), tm=_tile(n, 256))

    out = _experts(_bf(peer_down[0]), _bf(peer_up[0].T), h1tb, h1t, e1, cn, e2, r2,
                   r2d(ln2_g[0]), r2d(ln2_b[0]), tt=_tile(n, 512), ib=8)
    return out.reshape(batch, seq, d).astype(x.dtype)
```

```python
import functools
import math

import jax
import jax.numpy as jnp
import numpy as np
from jax import lax
from jax.experimental import pallas as pl
from jax.experimental.pallas import tpu as pltpu

F32 = jnp.float32
BF16 = jnp.bfloat16

CHUNK = 64
SUB = 16
HG_DK = 128
HG_HEADS = 4
S5_GROUP = 16
S5_STATE = 64
PEER_KEYS = 128
PEER_HEADS = 8
PEER_TOPK = 16
PEER_HALF = 128
LN_EPS = 1e-5
RMS_EPS = 1e-6
DN_ALPHA = 2.0 ** 0.25
NEG_INF = float("-inf")
LANES = 128
LOG2E = math.log2(math.e)

VMEM_LIMIT_BYTES = 56 * 1024 * 1024


def _params(*semantics, flags=None):
    return pltpu.CompilerParams(dimension_semantics=semantics,
                                vmem_limit_bytes=VMEM_LIMIT_BYTES, flags=flags)


def _bf(x):
    return x.astype(BF16)


def _layer_norm(x, g, b):
    mu = jnp.mean(x, axis=-1, keepdims=True)
    xc = x - mu
    var = jnp.mean(xc * xc, axis=-1, keepdims=True)
    return xc * lax.rsqrt(var + LN_EPS) * g + b


def _sigmoid(x):
    return 1.0 / (1.0 + jnp.exp(-x))


def _gelu(x):
    k0 = -2.0 * math.sqrt(2.0 / math.pi) * math.log2(math.e)
    w = x * (x * x * (k0 * 0.044715) + k0)
    return x / (1.0 + jnp.exp2(w))


def _inproj_kernel(x_ref, g_ref, b_ref, w_ref, lbl_ref,
                   h_ref, q_ref, gl_ref, k_ref, v_ref, gate_ref, u_ref, *, hw):
    h = _layer_norm(x_ref[...], g_ref[...], b_ref[...])
    h_ref[...] = h
    proj = jnp.dot(_bf(h), w_ref[...], preferred_element_type=F32)
    logits = lbl_ref[...]
    e = jnp.exp(logits - jnp.max(logits, axis=0, keepdims=True))
    lb = e[0:1, :] / jnp.sum(e, axis=0, keepdims=True)
    forget = lb + (1.0 - lb) * _sigmoid(proj[:, hw:2 * hw])
    q_ref[...] = proj[:, 0:hw] * (HG_DK ** -0.5)
    gl_ref[...] = jnp.log(forget)
    k_ref[...] = 1.0 - forget
    v_ref[...] = proj[:, 2 * hw:3 * hw]
    gate_ref[...] = proj[:, 3 * hw:4 * hw]
    u_ref[...] = proj[:, 4 * hw:]


def _inproj(x2, ln_g, ln_b, w_in_bf, lb_logits, *, tm):
    n, d = x2.shape
    cols = w_in_bf.shape[1]
    hw = lb_logits.shape[1]
    uw = cols - 4 * hw
    row = lambda w: pl.BlockSpec((tm, w), lambda i: (i, 0))
    full = lambda a: pl.BlockSpec(a.shape, lambda i: (0,) * a.ndim)
    outs = [jax.ShapeDtypeStruct((n, d), F32)] + \
           [jax.ShapeDtypeStruct((n, hw), F32)] * 5 + [jax.ShapeDtypeStruct((n, uw), F32)]
    return pl.pallas_call(
        functools.partial(_inproj_kernel, hw=hw),
        grid=(n // tm,),
        in_specs=[row(d), full(ln_g), full(ln_b), full(w_in_bf), full(lb_logits)],
        out_specs=[row(d)] + [row(hw)] * 5 + [row(uw)],
        out_shape=outs,
        compiler_params=_params("parallel"),
        name="inproj",
    )(x2, ln_g, ln_b, w_in_bf, lb_logits)


def _hgrn_masks():
    t = np.arange(CHUNK)
    le = t[None, :] <= t[:, None]
    same_sub = (t[None, :] // SUB) == (t[:, None] // SUB)
    m = np.concatenate([le, le & same_sub, same_sub], axis=0)
    return m.astype(np.float32)


def _diag_selector():
    s = np.repeat(np.arange(SUB), HG_DK)
    c = np.arange(CHUNK)
    return (s[:, None] == (c[None, :] % SUB)).astype(np.float32)


def _hgrn_kernel(q_ref, g_ref, k_ref, v_ref, gate_ref, ng_ref, m3_ref, sel_ref,
                 o_ref, state_ref, *, tc):
    nch = tc // CHUNK
    nsub = tc // SUB

    @pl.when(pl.program_id(1) == 0)
    def _():
        state_ref[...] = jnp.zeros_like(state_ref)

    g = g_ref[...]
    g_hi = _bf(g)
    r1 = g - g_hi.astype(F32)
    g_mid = _bf(r1)
    g_lo = _bf(r1 - g_mid.astype(F32))
    g_parts = jnp.concatenate([g_hi, g_mid, g_lo], axis=1)
    hw = g.shape[1]
    m3 = m3_ref[...]
    chunk_sums = []
    for c in range(nch):
        r = jnp.dot(m3, g_parts[c * CHUNK:(c + 1) * CHUNK], preferred_element_type=F32)
        chunk_sums.append(r[:, 0:hw] + r[:, hw:2 * hw] + r[:, 2 * hw:])
    part = lambda i: jnp.concatenate(
        [cs_[i * CHUNK:(i + 1) * CHUNK] for cs_ in chunk_sums], axis=0)
    cum_all, cs_all, tot_all = part(0), part(1), part(2)
    row = lax.broadcasted_iota(jnp.int32, (CHUNK, CHUNK), 0)
    col = lax.broadcasted_iota(jnp.int32, (CHUNK, CHUNK), 1)
    blk = row // SUB - col // SUB
    diag_ok = (blk == 0) & (col <= row)
    sel = sel_ref[...]
    ng = ng_ref[...]

    def prepare(h):
        lanes = slice(h * HG_DK, (h + 1) * HG_DK)
        q = q_ref[:, lanes]
        k = k_ref[:, lanes]
        cum = cum_all[:, lanes]
        cs = cs_all[:, lanes]
        tot = tot_all[:, lanes]
        cum3 = cum.reshape(nch, CHUNK, HG_DK)
        last = cum3[:, CHUNK - 1:CHUNK, :]
        q_inter = _bf(q * jnp.exp(cum))
        k_state = _bf((k.reshape(nch, CHUNK, HG_DK) * jnp.exp(last - cum3)).reshape(tc, HG_DK))
        q_sub = q * jnp.exp(cs)
        k_sub = _bf(k * jnp.exp(tot - cs))
        zeros1 = jnp.zeros((SUB, HG_DK), F32)
        tot_p1 = jnp.concatenate([zeros1, tot[:tc - SUB]], axis=0)
        tot_p2 = jnp.concatenate([zeros1, zeros1, tot[:tc - 2 * SUB]], axis=0)
        q_sub2 = q_sub * jnp.exp(tot_p1)
        q_sub3 = q_sub2 * jnp.exp(tot_p2)
        q_subs = (_bf(q_sub), _bf(q_sub2), _bf(q_sub3))
        cs3 = (cs * LOG2E).reshape(nsub, SUB, HG_DK)
        q3 = _bf(q).reshape(nsub, SUB, HG_DK)
        k3 = _bf(k).reshape(nsub, SUB, HG_DK)
        pieces = []
        for s in range(SUB):
            decay = jnp.exp2(jnp.minimum(cs3 - cs3[:, s:s + 1, :], 0.0))
            pieces.append((q3 * _bf(decay) * k3[:, s:s + 1, :]).reshape(tc, HG_DK))
        diag = jnp.dot(jnp.concatenate(pieces, axis=1), sel,
                       preferred_element_type=F32)
        return q_inter, k_state, q_subs, k_sub, diag, jnp.exp(last)

    def recur(h, prepared):
        q_inter, k_state, q_subs, k_sub, diag, decay_last = prepared
        lanes = slice(h * HG_DK, (h + 1) * HG_DK)
        v_bf = _bf(v_ref[:, lanes])
        st = state_ref[h]
        outs = []
        for c in range(nch):
            rows = slice(c * CHUNK, (c + 1) * CHUNK)
            inter = lax.dot_general(q_inter[rows], _bf(st), (((1,), (1,)), ((), ())),
                                    preferred_element_type=F32)
            lhs = jnp.concatenate([qs[rows] for qs in q_subs], axis=0)
            off = lax.dot_general(lhs, k_sub[rows], (((1,), (1,)), ((), ())),
                                  preferred_element_type=F32)
            sc = jnp.where(diag_ok, diag[rows],
                           jnp.where(blk == 1, off[0:CHUNK],
                                     jnp.where(blk == 2, off[CHUNK:2 * CHUNK],
                                               jnp.where(blk == 3, off[2 * CHUNK:], 0.0))))
            intra = jnp.dot(_bf(sc), v_bf[rows], preferred_element_type=F32)
            outs.append(inter + intra)
            upd = lax.dot_general(v_bf[rows], k_state[rows], (((0,), (0,)), ((), ())),
                                  preferred_element_type=F32)
            st = st * decay_last[c] + upd
        state_ref[h] = st
        o = jnp.concatenate(outs, axis=0)
        rms = lax.rsqrt(jnp.mean(o * o, axis=-1, keepdims=True) + RMS_EPS)
        gate = gate_ref[:, lanes]
        o_ref[:, lanes] = o * rms * ng * (gate * _sigmoid(gate))

    prepared = prepare(0)
    for h in range(HG_HEADS):
        nxt = prepare(h + 1) if h + 1 < HG_HEADS else None
        recur(h, prepared)
        prepared = nxt


def _hgrn(q, gl, k, v, gate, norm_g, *, batch, tc):
    n, hw = q.shape
    nblk = n // batch // tc
    m3 = jnp.asarray(_hgrn_masks(), dtype=BF16)
    sel = jnp.asarray(_diag_selector(), dtype=BF16)
    row = pl.BlockSpec((tc, hw), lambda b, s: (b * nblk + s, 0))
    full = lambda a: pl.BlockSpec(a.shape, lambda b, s: (0,) * a.ndim)
    return pl.pallas_call(
        functools.partial(_hgrn_kernel, tc=tc),
        grid=(batch, nblk),
        in_specs=[row] * 5 + [full(norm_g), full(m3), full(sel)],
        out_specs=row,
        out_shape=jax.ShapeDtypeStruct((n, hw), F32),
        scratch_shapes=[pltpu.VMEM((HG_HEADS, HG_DK, HG_DK), F32)],
        compiler_params=_params("parallel", "arbitrary"),
        name="hgrn2",
    )(q, gl, k, v, gate, norm_g, m3, sel)


def _time_major_perm(batch, tt):
    r = np.arange(batch * tt)
    p = np.zeros((batch * tt, batch * tt), np.float32)
    p[r, (r % batch) * tt + r // batch] = 1.0
    return p


def _s5_kernel(u_ref, perm_ref, permt_ref, ar_ref, ai_ref, wb_ref, wc_ref, d_ref, wg_ref, bg_ref,
               o_ref, bu_ref, st_ref, *, tt):
    batch, _, width = u_ref.shape
    ns = ar_ref.shape[1]
    half = ns // 2

    @pl.when(pl.program_id(0) == 0)
    def _():
        st_ref[...] = jnp.zeros_like(st_ref)

    u2 = u_ref[...].reshape(batch * tt, width)
    u_hi = _bf(u2)
    u_lo = _bf(u2 - u_hi.astype(F32))
    perm = perm_ref[...]
    u_hi_tm = jnp.dot(perm, u_hi, preferred_element_type=F32)
    u_tm = u_hi_tm + jnp.dot(perm, u_lo, preferred_element_type=F32)
    cb = 2 * LANES
    nblk = width // cb
    sb = ns // nblk
    for kb in range(nblk):
        ub = _bf(u_hi_tm[:, kb * cb:(kb + 1) * cb])
        for part in range(2):
            cols = slice(part * ns + kb * sb, part * ns + (kb + 1) * sb)
            bu_ref[:, cols] = jnp.dot(ub, wb_ref[kb * cb:(kb + 1) * cb, cols],
                                      preferred_element_type=F32)

    for part in range(2):
        lo = part * half
        ar = jnp.broadcast_to(ar_ref[:, lo:lo + half], (batch, half))
        ai = jnp.broadcast_to(ai_ref[:, lo:lo + half], (batch, half))

        def step(t, carry, lo=lo, ar=ar, ai=ai):
            xr, xi = carry
            rows = pl.ds(pl.multiple_of(t * batch, batch), batch)
            nr = ar * xr - ai * xi + bu_ref[rows, lo:lo + half]
            ni = ar * xi + ai * xr + bu_ref[rows, ns + lo:ns + lo + half]
            bu_ref[rows, lo:lo + half] = nr
            bu_ref[rows, ns + lo:ns + lo + half] = ni
            return nr, ni

        xr, xi = lax.fori_loop(0, tt, step, (st_ref[:, lo:lo + half],
                                             st_ref[:, ns + lo:ns + lo + half]))
        st_ref[:, lo:lo + half] = xr
        st_ref[:, ns + lo:ns + lo + half] = xi

    ys = []
    for ob in range(nblk):
        ocols = slice(ob * cb, (ob + 1) * cb)
        acc = None
        for part in range(2):
            rows = slice(part * ns + ob * sb, part * ns + (ob + 1) * sb)
            t = jnp.dot(_bf(bu_ref[:, rows]), wc_ref[rows, ocols], preferred_element_type=F32)
            acc = t if acc is None else acc + t
        ys.append(acc)
    y = jnp.concatenate(ys, axis=1) + d_ref[...] * u_tm
    z = _gelu(y)
    gl = jnp.dot(_bf(z), wg_ref[...], preferred_element_type=F32) + bg_ref[...]
    out_tm = _bf(z * _sigmoid(gl))
    out = jnp.dot(permt_ref[...], out_tm, preferred_element_type=F32)
    o_ref[...] = _bf(out).reshape(batch, tt, width)


def _s5(u3, abar_re, abar_im, wb, wc, d, wg, bg, *, tt):
    batch, seq, width = u3.shape
    ns = abar_re.shape[1]
    perm = _time_major_perm(batch, tt)
    perm_bf = jnp.asarray(perm, dtype=BF16)
    permt_bf = jnp.asarray(perm.T, dtype=BF16)
    blk = pl.BlockSpec((batch, tt, width), lambda s: (0, s, 0))
    full = lambda a: pl.BlockSpec(a.shape, lambda s: (0,) * a.ndim)
    consts = (perm_bf, permt_bf, abar_re, abar_im, wb, wc, d, wg, bg)
    return pl.pallas_call(
        functools.partial(_s5_kernel, tt=tt),
        grid=(seq // tt,),
        in_specs=[blk] + [full(a) for a in consts],
        out_specs=blk,
        out_shape=jax.ShapeDtypeStruct(u3.shape, BF16),
        scratch_shapes=[pltpu.VMEM((batch * tt, 2 * ns), F32),
                        pltpu.VMEM((batch, 2 * ns), F32)],
        compiler_params=_params("arbitrary"),
        name="s5",
    )(u3, *consts)


def _s5_discretise(a_re, a_im, log_step, b_re, b_im, c_re, c_im):
    groups, nstate, p = b_re.shape
    step = jnp.exp(log_step.astype(F32))[:, None]
    mag = jnp.exp(step * a_re)
    abar_re = mag * jnp.cos(step * a_im)
    abar_im = mag * jnp.sin(step * a_im)
    num_re, num_im = abar_re - 1.0, abar_im
    den = jnp.square(a_re) + jnp.square(a_im)
    coef_re = ((num_re * a_re + num_im * a_im) / den)[..., None]
    coef_im = ((num_im * a_re - num_re * a_im) / den)[..., None]
    bbar_re = coef_re * b_re - coef_im * b_im
    bbar_im = coef_re * b_im + coef_im * b_re
    eye = jnp.eye(groups, dtype=F32)
    bd = lambda m: jnp.einsum("gnp,gh->gphn", m, eye).reshape(groups * p, groups * nstate)
    wb = jnp.concatenate([bd(bbar_re), bd(bbar_im)], axis=1)
    cd = lambda m: jnp.einsum("gpn,gh->gnhp", m, eye).reshape(groups * nstate, groups * p)
    wc = jnp.concatenate([cd(c_re), -cd(c_im)], axis=0)
    flat = lambda m: m.reshape(1, groups * nstate)
    return flat(abar_re), flat(abar_im), _bf(wb), _bf(wc)


def _extract_top(s, want_rank):
    vals = []
    rank = jnp.full(s.shape, float(PEER_TOPK), F32)
    cur = s
    for r in range(PEER_TOPK):
        m = jnp.max(cur, axis=0, keepdims=True)
        vals.append(m)
        hit = cur == m
        if want_rank:
            rank = jnp.where(hit, float(r), rank)
        cur = jnp.where(hit, NEG_INF, cur)
    return vals, rank


def _stack_rows(rows, nrows):
    t = rows[0].shape[1]
    idx = lax.broadcasted_iota(jnp.int32, (nrows, t), 0)
    out = jnp.zeros((nrows, t), F32)
    for r, v in enumerate(rows):
        out = jnp.where(idx == r, v, out)
    return out


def _route_head(s1, s2):
    v1, _ = _extract_top(s1, False)
    v2, rank2 = _extract_top(s2, True)
    t = s1.shape[1]
    v2_lo = _stack_rows(v2[:8], 8)
    v2_hi = _stack_rows(v2[8:], 8)
    v1_hi = _stack_rows(v1[8:], 8)
    idx8 = lax.broadcasted_iota(jnp.int32, (8, t), 0)
    cand = [v1[0] + v2_lo, v1[0] + v2_hi]
    for a in range(1, 8):
        nb = PEER_TOPK // (a + 1)
        c = v1[a] + v2_lo
        cand.append(c if nb >= 8 else jnp.where(idx8 < nb, c, NEG_INF))
    cand.append(v1_hi + v2[0])
    cur = jnp.concatenate(cand, axis=0)
    top = v1[0] + v2[0]
    z = jnp.zeros_like(top)
    tau = top
    for _ in range(PEER_TOPK):
        tau = jnp.max(cur, axis=0, keepdims=True)
        z = z + jnp.exp(tau - top)
        cur = jnp.where(cur == tau, NEG_INF, cur)
    b0 = 4
    cnt = jnp.zeros(s1.shape, F32)
    for b in range(b0):
        cnt = cnt + jnp.where(s1 + v2[b] >= tau, 1.0, 0.0)
    for a in range(PEER_TOPK // (b0 + 1)):
        extra = jnp.zeros_like(tau)
        for b in range(b0, PEER_TOPK):
            extra = extra + jnp.where(v1[a] + v2[b] >= tau, 1.0, 0.0)
        cnt = cnt + jnp.where(s1 == v1[a], extra, 0.0)
    e1 = jnp.exp(s1 - v1[0])
    e2 = jnp.exp(s2 - v2[0]) / z
    return e1, cnt, e2, rank2


def _route_kernel(ma_ref, mb_ref, h0_ref, wo_ref, g_ref, b_ref, wq_ref, k1_ref, k2_ref,
                  h1t_ref, h1tb_ref, e1_ref, cn_ref, e2_ref, r2_ref):
    hw = ma_ref.shape[1]
    mixed = jnp.dot(_bf(ma_ref[...]), wo_ref[0:hw, :], preferred_element_type=F32)
    mixed = mixed + jnp.dot(_bf(mb_ref[...]), wo_ref[hw:, :], preferred_element_type=F32)
    h1 = _layer_norm(DN_ALPHA * h0_ref[...] + mixed, g_ref[...], b_ref[...])
    h1t = h1.T
    h1t_ref[...] = h1t
    h1tb = _bf(h1t)
    h1tb_ref[...] = h1tb
    qt = jnp.dot(wq_ref[...], h1tb, preferred_element_type=F32)
    k1 = k1_ref[...]
    k2 = k2_ref[...]
    for h in range(PEER_HEADS):
        base = h * 2 * PEER_HALF
        s1 = jnp.dot(k1, _bf(qt[base:base + PEER_HALF]), preferred_element_type=F32)
        s2 = jnp.dot(k2, _bf(qt[base + PEER_HALF:base + 2 * PEER_HALF]),
                     preferred_element_type=F32)
        e1, cnt, e2, rank2 = _route_head(s1, s2)
        e1_ref[h] = e1
        cn_ref[h] = cnt
        e2_ref[h] = _bf(e2)
        r2_ref[h] = _bf(rank2)


def _route(mix_a, mix_b, h0, w_out_bf, ln_g, ln_b, wq_t_bf, k1_bf, k2_bf, *, tm):
    n, d = h0.shape
    hw = mix_a.shape[1]
    row = lambda w: pl.BlockSpec((tm, w), lambda i: (i, 0))
    full = lambda a: pl.BlockSpec(a.shape, lambda i: (0,) * a.ndim)
    colblk = pl.BlockSpec((d, tm), lambda i: (0, i))
    tab = pl.BlockSpec((PEER_HEADS, PEER_KEYS, tm), lambda i: (0, 0, i))
    tab_shape = (PEER_HEADS, PEER_KEYS, n)
    return pl.pallas_call(
        _route_kernel,
        grid=(n // tm,),
        in_specs=[row(hw), row(hw), row(d)] +
                 [full(a) for a in (w_out_bf, ln_g, ln_b, wq_t_bf, k1_bf, k2_bf)],
        out_specs=[colblk, colblk, tab, tab, tab, tab],
        out_shape=[jax.ShapeDtypeStruct((d, n), F32), jax.ShapeDtypeStruct((d, n), BF16),
                   jax.ShapeDtypeStruct(tab_shape, F32), jax.ShapeDtypeStruct(tab_shape, F32),
                   jax.ShapeDtypeStruct(tab_shape, BF16), jax.ShapeDtypeStruct(tab_shape, BF16)],
        compiler_params=_params("parallel"),
        name="route",
    )(mix_a, mix_b, h0, w_out_bf, ln_g, ln_b, wq_t_bf, k1_bf, k2_bf)


def _experts_kernel(wd0_ref, wd_ref, wut_ref, h1tb_ref, h1t_ref, e1_ref, cn_ref, e2_ref, r2_ref,
                    g_ref, b_ref, o_ref, acc_ref, a0_ref, a1_ref, p_ref, *, ib, nk):
    s = pl.program_id(1)
    tt = h1tb_ref.shape[1]
    half = tt // 2

    @pl.when(s == 0)
    def _():
        acc_ref[...] = jnp.zeros_like(acc_ref)
        a0_ref[...] = jnp.dot(wd0_ref[...], h1tb_ref[...], preferred_element_type=F32)

    def step(a_cur, a_nxt):
        for hf in range(tt // half):
            hl = pl.ds(hf * half, half)
            for ii in range(ib):
                gsum = jnp.zeros((PEER_KEYS, half), BF16)
                for h in range(PEER_HEADS):
                    e1 = _bf(jnp.broadcast_to(e1_ref[h, ii:ii + 1, hl], gsum.shape))
                    cn = _bf(jnp.broadcast_to(cn_ref[h, ii:ii + 1, hl], gsum.shape))
                    val = e2_ref[h, :, hl] * e1
                    gsum = gsum + jnp.where(r2_ref[h, :, hl] < cn, val, jnp.zeros_like(val))
                rows = pl.ds(ii * PEER_KEYS, PEER_KEYS)
                p_ref[rows, hl] = _gelu(_bf(a_cur[rows, hl])) * gsum
            a_nxt[:, hl] = jnp.dot(wd_ref[...], h1tb_ref[:, hl], preferred_element_type=F32)
            acc_ref[:, hl] += jnp.dot(wut_ref[...], p_ref[:, hl], preferred_element_type=F32)

    @pl.when(s % 2 == 0)
    def _():
        step(a0_ref, a1_ref)

    @pl.when(s % 2 == 1)
    def _():
        step(a1_ref, a0_ref)

    @pl.when(s == nk - 1)
    def _():
        y = DN_ALPHA * h1t_ref[...] + acc_ref[...]
        mu = jnp.mean(y, axis=0, keepdims=True)
        yc = y - mu
        var = jnp.mean(yc * yc, axis=0, keepdims=True)
        yn = yc * lax.rsqrt(var + LN_EPS)
        o_ref[...] = yn.T * g_ref[...] + b_ref[...]


def _experts(wd_bf, wu_bf, h1tb, h1t, e1, cn, e2, r2, ln_g, ln_b, *, tt, ib):
    d, n = h1t.shape
    nexp = wd_bf.shape[0]
    eb = ib * PEER_KEYS
    nk = nexp // eb
    wut_bf = wu_bf.reshape(nk, eb, d).transpose(0, 2, 1)
    last = nk - 1
    tokcol = pl.BlockSpec((d, tt), lambda j, s: (0, j))
    tab_i = pl.BlockSpec((PEER_HEADS, ib, tt), lambda j, s: (0, s, j))
    tab_full = pl.BlockSpec((PEER_HEADS, PEER_KEYS, tt), lambda j, s: (0, 0, j))
    full = lambda a: pl.BlockSpec(a.shape, lambda j, s: (0,) * a.ndim)
    return pl.pallas_call(
        functools.partial(_experts_kernel, ib=ib, nk=nk),
        grid=(n // tt, nk),
        in_specs=[pl.BlockSpec((eb, d), lambda j, s: (0, 0)),
                  pl.BlockSpec((eb, d), lambda j, s: (jnp.minimum(s + 1, last), 0)),
                  pl.BlockSpec((None, d, eb), lambda j, s: (s, 0, 0)),
                  tokcol, tokcol, tab_i, tab_i, tab_full, tab_full, full(ln_g), full(ln_b)],
        out_specs=pl.BlockSpec((tt, d), lambda j, s: (j, 0)),
        out_shape=jax.ShapeDtypeStruct((n, d), F32),
        scratch_shapes=[pltpu.VMEM((d, tt), F32),
                        pltpu.VMEM((eb, tt), F32), pltpu.VMEM((eb, tt), F32),
                        pltpu.VMEM((eb, tt), BF16)],
        compiler_params=_params("parallel", "arbitrary"),
        name="experts",
    )(wd_bf, wd_bf, wut_bf, h1tb, h1t, e1, cn, e2, r2, ln_g, ln_b)


def _tile(n, pref):
    return pref if n % pref == 0 else n


def kernel(x, ln0_g, ln0_b, w_in, hg_lb_logits, hg_norm_g, s5_a_re, s5_a_im, s5_log_step,
           s5_b_re, s5_b_im, s5_c_re, s5_c_im, s5_d, w_glu, b_glu, w_out, ln1_g, ln1_b,
           w_query, peer_keys_1, peer_keys_2, peer_down, peer_up, ln2_g, ln2_b):
    assert w_in.shape[0] == 1, "single-layer trunk"
    batch, seq, d = x.shape
    n = batch * seq
    r2d = lambda a: a.reshape(1, -1).astype(F32)

    x2 = x.reshape(n, d).astype(F32)
    h0, q, gl, k, v, gate, u = _inproj(
        x2, r2d(ln0_g), r2d(ln0_b), _bf(w_in[0]), hg_lb_logits.astype(F32),
        tm=_tile(n, 512))

    mix_a = _hgrn(q, gl, k, v, gate, r2d(hg_norm_g[0]), batch=batch, tc=_tile(seq, 512))

    abar_re, abar_im, wb, wc = _s5_discretise(
        s5_a_re[0].astype(F32), s5_a_im[0].astype(F32), s5_log_step[0],
        s5_b_re[0].astype(F32), s5_b_im[0].astype(F32), s5_c_re[0], s5_c_im[0])
    mix_b = _s5(u.reshape(batch, seq, -1), abar_re, abar_im, wb, wc, r2d(s5_d[0]),
                _bf(w_glu[0]), r2d(b_glu[0]), tt=_tile(seq, 64)).reshape(n, -1)

    h1t, h1tb, e1, cn, e2, r2 = _route(
        mix_a, mix_b, h0, _bf(w_out[0]), r2d(ln1_g[0]), r2d(ln1_b[0]),
        _bf(w_query[0].T), _bf(peer_keys_1[0]), _bf(peer_keys_2[0]), tm=_tile(n, 256))

    out = _experts(_bf(peer_down[0]), _bf(peer_up[0]), h1tb, h1t, e1, cn, e2, r2,
                   r2d(ln2_g[0]), r2d(ln2_b[0]), tt=_tile(n, 512), ib=8)
    return out.reshape(batch, seq, d).astype(x.dtype)
```

```python
import functools
import math

import jax
import jax.numpy as jnp
import numpy as np
from jax import lax
from jax.experimental import pallas as pl
from jax.experimental.pallas import tpu as pltpu

F32 = jnp.float32
BF16 = jnp.bfloat16

CHUNK = 64
SUB = 16
HG_DK = 128
HG_HEADS = 4
S5_GROUP = 16
S5_STATE = 64
PEER_KEYS = 128
PEER_HEADS = 8
PEER_TOPK = 16
PEER_HALF = 128
LN_EPS = 1e-5
RMS_EPS = 1e-6
DN_ALPHA = 2.0 ** 0.25
NEG_INF = float("-inf")
LANES = 128
SUBLANES = 8
LOG2E = math.log2(math.e)

VMEM_LIMIT_BYTES = 56 * 1024 * 1024


def _params(*semantics, flags=None):
    return pltpu.CompilerParams(dimension_semantics=semantics,
                                vmem_limit_bytes=VMEM_LIMIT_BYTES, flags=flags)


def _bf(x):
    return x.astype(BF16)


def _layer_norm(x, g, b):
    mu = jnp.mean(x, axis=-1, keepdims=True)
    xc = x - mu
    var = jnp.mean(xc * xc, axis=-1, keepdims=True)
    return xc * lax.rsqrt(var + LN_EPS) * g + b


def _sigmoid(x):
    return 1.0 / (1.0 + jnp.exp(-x))


def _gelu(x):
    k0 = -2.0 * math.sqrt(2.0 / math.pi) * math.log2(math.e)
    w = x * (x * x * (k0 * 0.044715) + k0)
    return x / (1.0 + jnp.exp2(w))


def _inproj_kernel(x_ref, g_ref, b_ref, w_ref, lbl_ref,
                   h_ref, q_ref, gl_ref, k_ref, v_ref, gate_ref, u_ref, *, hw):
    h = _layer_norm(x_ref[...], g_ref[...], b_ref[...])
    h_ref[...] = h
    proj = jnp.dot(_bf(h), w_ref[...], preferred_element_type=F32)
    logits = lbl_ref[...]
    e = jnp.exp(logits - jnp.max(logits, axis=0, keepdims=True))
    lb = e[0:1, :] / jnp.sum(e, axis=0, keepdims=True)
    forget = lb + (1.0 - lb) * _sigmoid(proj[:, hw:2 * hw])
    q_ref[...] = proj[:, 0:hw] * (HG_DK ** -0.5)
    gl_ref[...] = jnp.log(forget)
    k_ref[...] = 1.0 - forget
    v_ref[...] = proj[:, 2 * hw:3 * hw]
    gate_ref[...] = proj[:, 3 * hw:4 * hw]
    u_ref[...] = proj[:, 4 * hw:]


def _inproj(x2, ln_g, ln_b, w_in_bf, lb_logits, *, tm):
    n, d = x2.shape
    cols = w_in_bf.shape[1]
    hw = lb_logits.shape[1]
    uw = cols - 4 * hw
    row = lambda w: pl.BlockSpec((tm, w), lambda i: (i, 0))
    full = lambda a: pl.BlockSpec(a.shape, lambda i: (0,) * a.ndim)
    outs = [jax.ShapeDtypeStruct((n, d), F32)] + \
           [jax.ShapeDtypeStruct((n, hw), F32)] * 5 + [jax.ShapeDtypeStruct((n, uw), F32)]
    return pl.pallas_call(
        functools.partial(_inproj_kernel, hw=hw),
        grid=(n // tm,),
        in_specs=[row(d), full(ln_g), full(ln_b), full(w_in_bf), full(lb_logits)],
        out_specs=[row(d)] + [row(hw)] * 5 + [row(uw)],
        out_shape=outs,
        compiler_params=_params("parallel"),
        name="inproj",
    )(x2, ln_g, ln_b, w_in_bf, lb_logits)


def _hgrn_masks():
    t = np.arange(CHUNK)
    le = t[None, :] <= t[:, None]
    same_sub = (t[None, :] // SUB) == (t[:, None] // SUB)
    m = np.concatenate([le, le & same_sub, same_sub], axis=0)
    return m.astype(np.float32)


def _diag_selector():
    s = np.repeat(np.arange(SUB), HG_DK)
    c = np.arange(CHUNK)
    return (s[:, None] == (c[None, :] % SUB)).astype(np.float32)


def _hgrn_kernel(q_ref, g_ref, k_ref, v_ref, gate_ref, ng_ref, m3_ref, sel_ref,
                 o_ref, state_ref, *, tc):
    nch = tc // CHUNK
    nsub = tc // SUB

    @pl.when(pl.program_id(1) == 0)
    def _():
        state_ref[...] = jnp.zeros_like(state_ref)

    g = g_ref[...]
    g_hi = _bf(g)
    r1 = g - g_hi.astype(F32)
    g_mid = _bf(r1)
    g_lo = _bf(r1 - g_mid.astype(F32))
    g_parts = jnp.concatenate([g_hi, g_mid, g_lo], axis=1)
    hw = g.shape[1]
    m3 = m3_ref[...]
    chunk_sums = []
    for c in range(nch):
        r = jnp.dot(m3, g_parts[c * CHUNK:(c + 1) * CHUNK], preferred_element_type=F32)
        chunk_sums.append(r[:, 0:hw] + r[:, hw:2 * hw] + r[:, 2 * hw:])
    part = lambda i: jnp.concatenate(
        [cs_[i * CHUNK:(i + 1) * CHUNK] for cs_ in chunk_sums], axis=0)
    cum_all, cs_all, tot_all = part(0), part(1), part(2)
    row = lax.broadcasted_iota(jnp.int32, (CHUNK, CHUNK), 0)
    col = lax.broadcasted_iota(jnp.int32, (CHUNK, CHUNK), 1)
    blk = row // SUB - col // SUB
    diag_ok = (blk == 0) & (col <= row)
    sel = sel_ref[...]
    ng = ng_ref[...]

    def prepare(h):
        lanes = slice(h * HG_DK, (h + 1) * HG_DK)
        q = q_ref[:, lanes]
        k = k_ref[:, lanes]
        cum = cum_all[:, lanes]
        cs = cs_all[:, lanes]
        tot = tot_all[:, lanes]
        cum3 = cum.reshape(nch, CHUNK, HG_DK)
        last = cum3[:, CHUNK - 1:CHUNK, :]
        q_inter = _bf(q * jnp.exp(cum))
        k_state = _bf((k.reshape(nch, CHUNK, HG_DK) * jnp.exp(last - cum3)).reshape(tc, HG_DK))
        q_sub = q * jnp.exp(cs)
        k_sub = _bf(k * jnp.exp(tot - cs))
        zeros1 = jnp.zeros((SUB, HG_DK), F32)
        tot_p1 = jnp.concatenate([zeros1, tot[:tc - SUB]], axis=0)
        tot_p2 = jnp.concatenate([zeros1, zeros1, tot[:tc - 2 * SUB]], axis=0)
        q_sub2 = q_sub * jnp.exp(tot_p1)
        q_sub3 = q_sub2 * jnp.exp(tot_p2)
        q_subs = (_bf(q_sub), _bf(q_sub2), _bf(q_sub3))
        cs3 = (cs * LOG2E).reshape(nsub, SUB, HG_DK)
        q3 = _bf(q).reshape(nsub, SUB, HG_DK)
        k3 = _bf(k).reshape(nsub, SUB, HG_DK)
        pieces = []
        for s in range(SUB):
            decay = jnp.exp2(jnp.minimum(cs3 - cs3[:, s:s + 1, :], 0.0))
            pieces.append((q3 * _bf(decay) * k3[:, s:s + 1, :]).reshape(tc, HG_DK))
        diag = jnp.dot(jnp.concatenate(pieces, axis=1), sel,
                       preferred_element_type=F32)
        return q_inter, k_state, q_subs, k_sub, diag, jnp.exp(last)

    def recur(h, prepared):
        q_inter, k_state, q_subs, k_sub, diag, decay_last = prepared
        lanes = slice(h * HG_DK, (h + 1) * HG_DK)
        v_bf = _bf(v_ref[:, lanes])
        st = state_ref[h]
        outs = []
        for c in range(nch):
            rows = slice(c * CHUNK, (c + 1) * CHUNK)
            inter = lax.dot_general(q_inter[rows], _bf(st), (((1,), (1,)), ((), ())),
                                    preferred_element_type=F32)
            lhs = jnp.concatenate([qs[rows] for qs in q_subs], axis=0)
            off = lax.dot_general(lhs, k_sub[rows], (((1,), (1,)), ((), ())),
                                  preferred_element_type=F32)
            sc = jnp.where(diag_ok, diag[rows],
                           jnp.where(blk == 1, off[0:CHUNK],
                                     jnp.where(blk == 2, off[CHUNK:2 * CHUNK],
                                               jnp.where(blk == 3, off[2 * CHUNK:], 0.0))))
            intra = jnp.dot(_bf(sc), v_bf[rows], preferred_element_type=F32)
            outs.append(inter + intra)
            upd = lax.dot_general(v_bf[rows], k_state[rows], (((0,), (0,)), ((), ())),
                                  preferred_element_type=F32)
            st = st * decay_last[c] + upd
        state_ref[h] = st
        o = jnp.concatenate(outs, axis=0)
        rms = lax.rsqrt(jnp.mean(o * o, axis=-1, keepdims=True) + RMS_EPS)
        gate = gate_ref[:, lanes]
        o_ref[:, lanes] = o * rms * ng * (gate * _sigmoid(gate))

    prepared = prepare(0)
    for h in range(HG_HEADS):
        nxt = prepare(h + 1) if h + 1 < HG_HEADS else None
        recur(h, prepared)
        prepared = nxt


def _hgrn(q, gl, k, v, gate, norm_g, *, batch, tc):
    n, hw = q.shape
    nblk = n // batch // tc
    m3 = jnp.asarray(_hgrn_masks(), dtype=BF16)
    sel = jnp.asarray(_diag_selector(), dtype=BF16)
    row = pl.BlockSpec((tc, hw), lambda b, s: (b * nblk + s, 0))
    full = lambda a: pl.BlockSpec(a.shape, lambda b, s: (0,) * a.ndim)
    return pl.pallas_call(
        functools.partial(_hgrn_kernel, tc=tc),
        grid=(batch, nblk),
        in_specs=[row] * 5 + [full(norm_g), full(m3), full(sel)],
        out_specs=row,
        out_shape=jax.ShapeDtypeStruct((n, hw), F32),
        scratch_shapes=[pltpu.VMEM((HG_HEADS, HG_DK, HG_DK), F32)],
        compiler_params=_params("parallel", "arbitrary"),
        name="hgrn2",
    )(q, gl, k, v, gate, norm_g, m3, sel)


def _time_major_perm(batch, tt):
    r = np.arange(batch * tt)
    p = np.zeros((batch * tt, batch * tt), np.float32)
    p[r, (r % batch) * tt + r // batch] = 1.0
    return p


def _s5_kernel(u_ref, perm_ref, permt_ref, ar_ref, ai_ref, wb_ref, wc_ref, d_ref, wg_ref, bg_ref,
               o_ref, bu_ref, st_ref, *, tt):
    batch, _, width = u_ref.shape
    ns = ar_ref.shape[1]
    half = ns // 2

    @pl.when(pl.program_id(0) == 0)
    def _():
        st_ref[...] = jnp.zeros_like(st_ref)

    u2 = u_ref[...].reshape(batch * tt, width)
    u_hi = _bf(u2)
    u_lo = _bf(u2 - u_hi.astype(F32))
    perm = perm_ref[...]
    u_hi_tm = jnp.dot(perm, u_hi, preferred_element_type=F32)
    u_tm = u_hi_tm + jnp.dot(perm, u_lo, preferred_element_type=F32)
    cb = 2 * LANES
    nblk = width // cb
    sb = ns // nblk
    for kb in range(nblk):
        ub = _bf(u_hi_tm[:, kb * cb:(kb + 1) * cb])
        for part in range(2):
            cols = slice(part * ns + kb * sb, part * ns + (kb + 1) * sb)
            bu_ref[:, cols] = jnp.dot(ub, wb_ref[kb * cb:(kb + 1) * cb, cols],
                                      preferred_element_type=F32)

    for part in range(2):
        lo = part * half
        ar = jnp.broadcast_to(ar_ref[:, lo:lo + half], (batch, half))
        ai = jnp.broadcast_to(ai_ref[:, lo:lo + half], (batch, half))

        def step(t, carry, lo=lo, ar=ar, ai=ai):
            xr, xi = carry
            rows = pl.ds(pl.multiple_of(t * batch, batch), batch)
            nr = ar * xr - ai * xi + bu_ref[rows, lo:lo + half]
            ni = ar * xi + ai * xr + bu_ref[rows, ns + lo:ns + lo + half]
            bu_ref[rows, lo:lo + half] = nr
            bu_ref[rows, ns + lo:ns + lo + half] = ni
            return nr, ni

        xr, xi = lax.fori_loop(0, tt, step, (st_ref[:, lo:lo + half],
                                             st_ref[:, ns + lo:ns + lo + half]))
        st_ref[:, lo:lo + half] = xr
        st_ref[:, ns + lo:ns + lo + half] = xi

    ys = []
    for ob in range(nblk):
        ocols = slice(ob * cb, (ob + 1) * cb)
        acc = None
        for part in range(2):
            rows = slice(part * ns + ob * sb, part * ns + (ob + 1) * sb)
            t = jnp.dot(_bf(bu_ref[:, rows]), wc_ref[rows, ocols], preferred_element_type=F32)
            acc = t if acc is None else acc + t
        ys.append(acc)
    y = jnp.concatenate(ys, axis=1) + d_ref[...] * u_tm
    z = _gelu(y)
    gl = jnp.dot(_bf(z), wg_ref[...], preferred_element_type=F32) + bg_ref[...]
    out_tm = _bf(z * _sigmoid(gl))
    out = jnp.dot(permt_ref[...], out_tm, preferred_element_type=F32)
    o_ref[...] = _bf(out).reshape(batch, tt, width)


def _s5(u3, abar_re, abar_im, wb, wc, d, wg, bg, *, tt):
    batch, seq, width = u3.shape
    ns = abar_re.shape[1]
    perm = _time_major_perm(batch, tt)
    perm_bf = jnp.asarray(perm, dtype=BF16)
    permt_bf = jnp.asarray(perm.T, dtype=BF16)
    blk = pl.BlockSpec((batch, tt, width), lambda s: (0, s, 0))
    full = lambda a: pl.BlockSpec(a.shape, lambda s: (0,) * a.ndim)
    consts = (perm_bf, permt_bf, abar_re, abar_im, wb, wc, d, wg, bg)
    return pl.pallas_call(
        functools.partial(_s5_kernel, tt=tt),
        grid=(seq // tt,),
        in_specs=[blk] + [full(a) for a in consts],
        out_specs=blk,
        out_shape=jax.ShapeDtypeStruct(u3.shape, BF16),
        scratch_shapes=[pltpu.VMEM((batch * tt, 2 * ns), F32),
                        pltpu.VMEM((batch, 2 * ns), F32)],
        compiler_params=_params("arbitrary"),
        name="s5",
    )(u3, *consts)


def _s5_discretise(a_re, a_im, log_step, b_re, b_im, c_re, c_im):
    groups, nstate, p = b_re.shape
    step = jnp.exp(log_step.astype(F32))[:, None]
    mag = jnp.exp(step * a_re)
    abar_re = mag * jnp.cos(step * a_im)
    abar_im = mag * jnp.sin(step * a_im)
    num_re, num_im = abar_re - 1.0, abar_im
    den = jnp.square(a_re) + jnp.square(a_im)
    coef_re = ((num_re * a_re + num_im * a_im) / den)[..., None]
    coef_im = ((num_im * a_re - num_re * a_im) / den)[..., None]
    bbar_re = coef_re * b_re - coef_im * b_im
    bbar_im = coef_re * b_im + coef_im * b_re
    eye = jnp.eye(groups, dtype=F32)
    bd = lambda m: jnp.einsum("gnp,gh->gphn", m, eye).reshape(groups * p, groups * nstate)
    wb = jnp.concatenate([bd(bbar_re), bd(bbar_im)], axis=1)
    cd = lambda m: jnp.einsum("gpn,gh->gnhp", m, eye).reshape(groups * nstate, groups * p)
    wc = jnp.concatenate([cd(c_re), -cd(c_im)], axis=0)
    flat = lambda m: m.reshape(1, groups * nstate)
    return flat(abar_re), flat(abar_im), _bf(wb), _bf(wc)


def _oddeven_merge_sort(n):
    pairs = []

    def merge(lo, hi, r):
        step = r * 2
        if step < hi - lo:
            merge(lo, hi, step)
            merge(lo + r, hi, step)
            pairs.extend((i, i + r) for i in range(lo + r, hi - r, step))
        else:
            pairs.append((lo, lo + r))

    def sort(lo, hi):
        if hi - lo >= 1:
            mid = lo + (hi - lo) // 2
            sort(lo, mid)
            sort(mid + 1, hi)
            merge(lo, hi, 1)

    sort(0, n - 1)
    return pairs


def _bitonic_merge(n):
    pairs = []
    dist = n // 2
    while dist >= 1:
        pairs.extend((i, i + dist) for i in range(n) if not i & dist)
        dist //= 2
    return pairs


def _compare_exchange(xs, pairs):
    for i, j in pairs:
        xs[i], xs[j] = jnp.maximum(xs[i], xs[j]), jnp.minimum(xs[i], xs[j])


def _sorted_top(s):
    n = s.shape[0] // SUBLANES
    assert n == PEER_TOPK and n & (n - 1) == 0
    xs = [s[g * SUBLANES:(g + 1) * SUBLANES, :] for g in range(n)]
    _compare_exchange(xs, _oddeven_merge_sort(n))
    shift = SUBLANES // 2
    while shift >= 1:
        rolled = [pltpu.roll(x, shift, 0) for x in xs]
        xs = [jnp.maximum(xs[r], rolled[n - 1 - r]) for r in range(n)]
        _compare_exchange(xs, _bitonic_merge(n))
        shift //= 2
    return [x[0:1, :] for x in xs]


def _rank_among(vals, s):
    rank = jnp.full(s.shape, float(len(vals)), F32)
    for b in range(len(vals) - 1, -1, -1):
        rank = jnp.where(vals[b] > s, rank, float(b))
    return rank


def _stack_rows(rows, nrows):
    t = rows[0].shape[1]
    idx = lax.broadcasted_iota(jnp.int32, (nrows, t), 0)
    out = jnp.zeros((nrows, t), F32)
    for r, v in enumerate(rows):
        out = jnp.where(idx == r, v, out)
    return out


def _route_head(s1, s2):
    v1 = _sorted_top(s1)
    v2 = _sorted_top(s2)
    rank2 = _rank_among(v2, s2)
    t = s1.shape[1]
    v2_lo = _stack_rows(v2[:8], 8)
    v2_hi = _stack_rows(v2[8:], 8)
    v1_hi = _stack_rows(v1[8:], 8)
    idx8 = lax.broadcasted_iota(jnp.int32, (8, t), 0)
    cand = [v1[0] + v2_lo, v1[0] + v2_hi]
    for a in range(1, 8):
        nb = PEER_TOPK // (a + 1)
        c = v1[a] + v2_lo
        cand.append(c if nb >= 8 else jnp.where(idx8 < nb, c, NEG_INF))
    cand.append(v1_hi + v2[0])
    cur = jnp.concatenate(cand, axis=0)
    top = v1[0] + v2[0]
    z = jnp.zeros_like(top)
    tau = top
    for _ in range(PEER_TOPK):
        tau = jnp.max(cur, axis=0, keepdims=True)
        z = z + jnp.exp(tau - top)
        cur = jnp.where(cur == tau, NEG_INF, cur)
    b0 = 4
    cnt = jnp.zeros(s1.shape, F32)
    for b in range(b0):
        cnt = cnt + jnp.where(s1 + v2[b] >= tau, 1.0, 0.0)
    for a in range(PEER_TOPK // (b0 + 1)):
        extra = jnp.zeros_like(tau)
        for b in range(b0, PEER_TOPK):
            extra = extra + jnp.where(v1[a] + v2[b] >= tau, 1.0, 0.0)
        cnt = cnt + jnp.where(s1 == v1[a], extra, 0.0)
    e1 = jnp.exp(s1 - v1[0])
    e2 = jnp.exp(s2 - v2[0]) / z
    return e1, cnt, e2, rank2


def _route_kernel(ma_ref, mb_ref, h0_ref, wo_ref, g_ref, b_ref, wq_ref, k1_ref, k2_ref,
                  h1t_ref, h1tb_ref, e1_ref, cn_ref, e2_ref, r2_ref):
    hw = ma_ref.shape[1]
    mixed = jnp.dot(_bf(ma_ref[...]), wo_ref[0:hw, :], preferred_element_type=F32)
    mixed = mixed + jnp.dot(_bf(mb_ref[...]), wo_ref[hw:, :], preferred_element_type=F32)
    h1 = _layer_norm(DN_ALPHA * h0_ref[...] + mixed, g_ref[...], b_ref[...])
    h1t = h1.T
    h1t_ref[...] = h1t
    h1tb = _bf(h1t)
    h1tb_ref[...] = h1tb
    qt = jnp.dot(wq_ref[...], h1tb, preferred_element_type=F32)
    k1 = k1_ref[...]
    k2 = k2_ref[...]
    for h in range(PEER_HEADS):
        base = h * 2 * PEER_HALF
        s1 = jnp.dot(k1, _bf(qt[base:base + PEER_HALF]), preferred_element_type=F32)
        s2 = jnp.dot(k2, _bf(qt[base + PEER_HALF:base + 2 * PEER_HALF]),
                     preferred_element_type=F32)
        e1, cnt, e2, rank2 = _route_head(s1, s2)
        e1_ref[h] = e1
        cn_ref[h] = cnt
        e2_ref[h] = _bf(e2)
        r2_ref[h] = _bf(rank2)


def _route(mix_a, mix_b, h0, w_out_bf, ln_g, ln_b, wq_t_bf, k1_bf, k2_bf, *, tm):
    n, d = h0.shape
    hw = mix_a.shape[1]
    row = lambda w: pl.BlockSpec((tm, w), lambda i: (i, 0))
    full = lambda a: pl.BlockSpec(a.shape, lambda i: (0,) * a.ndim)
    colblk = pl.BlockSpec((d, tm), lambda i: (0, i))
    tab = pl.BlockSpec((PEER_HEADS, PEER_KEYS, tm), lambda i: (0, 0, i))
    tab_shape = (PEER_HEADS, PEER_KEYS, n)
    return pl.pallas_call(
        _route_kernel,
        grid=(n // tm,),
        in_specs=[row(hw), row(hw), row(d)] +
                 [full(a) for a in (w_out_bf, ln_g, ln_b, wq_t_bf, k1_bf, k2_bf)],
        out_specs=[colblk, colblk, tab, tab, tab, tab],
        out_shape=[jax.ShapeDtypeStruct((d, n), F32), jax.ShapeDtypeStruct((d, n), BF16),
                   jax.ShapeDtypeStruct(tab_shape, F32), jax.ShapeDtypeStruct(tab_shape, F32),
                   jax.ShapeDtypeStruct(tab_shape, BF16), jax.ShapeDtypeStruct(tab_shape, BF16)],
        compiler_params=_params("parallel"),
        name="route",
    )(mix_a, mix_b, h0, w_out_bf, ln_g, ln_b, wq_t_bf, k1_bf, k2_bf)


def _experts_kernel(wd0_ref, wd_ref, wut_ref, h1tb_ref, h1t_ref, e1_ref, cn_ref, e2_ref, r2_ref,
                    g_ref, b_ref, o_ref, acc_ref, a0_ref, a1_ref, p_ref, *, ib, nk):
    s = pl.program_id(1)
    tt = h1tb_ref.shape[1]
    half = tt // 2

    @pl.when(s == 0)
    def _():
        acc_ref[...] = jnp.zeros_like(acc_ref)
        a0_ref[...] = jnp.dot(wd0_ref[...], h1tb_ref[...], preferred_element_type=F32)

    def step(a_cur, a_nxt):
        for hf in range(tt // half):
            hl = pl.ds(hf * half, half)
            for ii in range(ib):
                gsum = jnp.zeros((PEER_KEYS, half), BF16)
                for h in range(PEER_HEADS):
                    e1 = _bf(jnp.broadcast_to(e1_ref[h, ii:ii + 1, hl], gsum.shape))
                    cn = _bf(jnp.broadcast_to(cn_ref[h, ii:ii + 1, hl], gsum.shape))
                    val = e2_ref[h, :, hl] * e1
                    gsum = gsum + jnp.where(r2_ref[h, :, hl] < cn, val, jnp.zeros_like(val))
                rows = pl.ds(ii * PEER_KEYS, PEER_KEYS)
                p_ref[rows, hl] = _gelu(_bf(a_cur[rows, hl])) * gsum
            a_nxt[:, hl] = jnp.dot(wd_ref[...], h1tb_ref[:, hl], preferred_element_type=F32)
            acc_ref[:, hl] += jnp.dot(wut_ref[...], p_ref[:, hl], preferred_element_type=F32)

    @pl.when(s % 2 == 0)
    def _():
        step(a0_ref, a1_ref)

    @pl.when(s % 2 == 1)
    def _():
        step(a1_ref, a0_ref)

    @pl.when(s == nk - 1)
    def _():
        y = DN_ALPHA * h1t_ref[...] + acc_ref[...]
        mu = jnp.mean(y, axis=0, keepdims=True)
        yc = y - mu
        var = jnp.mean(yc * yc, axis=0, keepdims=True)
        yn = yc * lax.rsqrt(var + LN_EPS)
        o_ref[...] = yn.T * g_ref[...] + b_ref[...]


def _experts(wd_bf, wu_bf, h1tb, h1t, e1, cn, e2, r2, ln_g, ln_b, *, tt, ib):
    d, n = h1t.shape
    nexp = wd_bf.shape[0]
    eb = ib * PEER_KEYS
    nk = nexp // eb
    wut_bf = wu_bf.reshape(nk, eb, d).transpose(0, 2, 1)
    last = nk - 1
    tokcol = pl.BlockSpec((d, tt), lambda j, s: (0, j))
    tab_i = pl.BlockSpec((PEER_HEADS, ib, tt), lambda j, s: (0, s, j))
    tab_full = pl.BlockSpec((PEER_HEADS, PEER_KEYS, tt), lambda j, s: (0, 0, j))
    full = lambda a: pl.BlockSpec(a.shape, lambda j, s: (0,) * a.ndim)
    return pl.pallas_call(
        functools.partial(_experts_kernel, ib=ib, nk=nk),
        grid=(n // tt, nk),
        in_specs=[pl.BlockSpec((eb, d), lambda j, s: (0, 0)),
                  pl.BlockSpec((eb, d), lambda j, s: (jnp.minimum(s + 1, last), 0)),
                  pl.BlockSpec((None, d, eb), lambda j, s: (s, 0, 0)),
                  tokcol, tokcol, tab_i, tab_i, tab_full, tab_full, full(ln_g), full(ln_b)],
        out_specs=pl.BlockSpec((tt, d), lambda j, s: (j, 0)),
        out_shape=jax.ShapeDtypeStruct((n, d), F32),
        scratch_shapes=[pltpu.VMEM((d, tt), F32),
                        pltpu.VMEM((eb, tt), F32), pltpu.VMEM((eb, tt), F32),
                        pltpu.VMEM((eb, tt), BF16)],
        compiler_params=_params("parallel", "arbitrary"),
        name="experts",
    )(wd_bf, wd_bf, wut_bf, h1tb, h1t, e1, cn, e2, r2, ln_g, ln_b)


def _tile(n, pref):
    return pref if n % pref == 0 else n


def kernel(x, ln0_g, ln0_b, w_in, hg_lb_logits, hg_norm_g, s5_a_re, s5_a_im, s5_log_step,
           s5_b_re, s5_b_im, s5_c_re, s5_c_im, s5_d, w_glu, b_glu, w_out, ln1_g, ln1_b,
           w_query, peer_keys_1, peer_keys_2, peer_down, peer_up, ln2_g, ln2_b):
    assert w_in.shape[0] == 1, "single-layer trunk"
    batch, seq, d = x.shape
    n = batch * seq
    r2d = lambda a: a.reshape(1, -1).astype(F32)

    x2 = x.reshape(n, d).astype(F32)
    h0, q, gl, k, v, gate, u = _inproj(
        x2, r2d(ln0_g), r2d(ln0_b), _bf(w_in[0]), hg_lb_logits.astype(F32),
        tm=_tile(n, 512))

    mix_a = _hgrn(q, gl, k, v, gate, r2d(hg_norm_g[0]), batch=batch, tc=_tile(seq, 512))

    abar_re, abar_im, wb, wc = _s5_discretise(
        s5_a_re[0].astype(F32), s5_a_im[0].astype(F32), s5_log_step[0],
        s5_b_re[0].astype(F32), s5_b_im[0].astype(F32), s5_c_re[0], s5_c_im[0])
    mix_b = _s5(u.reshape(batch, seq, -1), abar_re, abar_im, wb, wc, r2d(s5_d[0]),
                _bf(w_glu[0]), r2d(b_glu[0]), tt=_tile(seq, 64)).reshape(n, -1)

    h1t, h1tb, e1, cn, e2, r2 = _route(
        mix_a, mix_b, h0, _bf(w_out[0]), r2d(ln1_g[0]), r2d(ln1_b[0]),
        _bf(w_query[0].T), _bf(peer_keys_1[0]), _bf(peer_keys_2[0]), tm=_tile(n, 256))

    out = _experts(_bf(peer_down[0]), _bf(peer_up[0]), h1tb, h1t, e1, cn, e2, r2,
                   r2d(ln2_g[0]), r2d(ln2_b[0]), tt=_tile(n, 512), ib=8)
    return out.reshape(batch, seq, d).astype(x.dtype)
```

```python
import functools
import math
from typing import NamedTuple

import jax
import jax.numpy as jnp
import numpy as np
from jax import lax
from jax.experimental import pallas as pl
from jax.experimental.pallas import tpu as pltpu

F32 = jnp.float32
BF16 = jnp.bfloat16

CHUNK = 64
SUB = 16
HG_DK = 128
HG_HEADS = 4
PEER_KEYS = 128
PEER_HEADS = 8
PEER_TOPK = 16
PEER_HALF = 128
LN_EPS = 1e-5
RMS_EPS = 1e-6
DN_ALPHA = 2.0 ** 0.25
NEG_INF = float("-inf")
LANES = 128
SUBLANES = 8

VMEM_LIMIT_BYTES = 56 * 1024 * 1024


def _params(*semantics):
    return pltpu.CompilerParams(dimension_semantics=semantics,
                                vmem_limit_bytes=VMEM_LIMIT_BYTES)


def _bf(x):
    return x.astype(BF16)


def _layer_norm(x, g, b):
    mu = jnp.mean(x, axis=-1, keepdims=True)
    xc = x - mu
    var = jnp.mean(xc * xc, axis=-1, keepdims=True)
    return xc * lax.rsqrt(var + LN_EPS) * g + b


def _sigmoid(x):
    return 1.0 / (1.0 + jnp.exp(-x))


def _gelu_tanh(x):
    c = math.sqrt(2.0 / math.pi)
    return x * (0.5 * (1.0 + jnp.tanh(c * (x + 0.044715 * (x * x * x)))))


def _gelu(x):
    k0 = -2.0 * math.sqrt(2.0 / math.pi) * math.log2(math.e)
    w = x * (x * x * (k0 * 0.044715) + k0)
    return x / (1.0 + jnp.exp2(w))


def _inproj_kernel(x_ref, g_ref, b_ref, w_ref, lbl_ref,
                   h_ref, q_ref, gl_ref, k_ref, v_ref, gate_ref, u_ref, *, hw):
    h = _layer_norm(x_ref[...], g_ref[...], b_ref[...])
    h_ref[...] = h
    proj = jnp.dot(_bf(h), w_ref[...], preferred_element_type=F32)
    logits = lbl_ref[...]
    e = jnp.exp(logits - jnp.max(logits, axis=0, keepdims=True))
    lb = e[0:1, :] / jnp.sum(e, axis=0, keepdims=True)
    forget = lb + (1.0 - lb) * _sigmoid(proj[:, hw:2 * hw])
    q_ref[...] = proj[:, 0:hw] * (HG_DK ** -0.5)
    gl_ref[...] = jnp.log(forget)
    k_ref[...] = 1.0 - forget
    v_ref[...] = _bf(proj[:, 2 * hw:3 * hw])
    gate_ref[...] = proj[:, 3 * hw:4 * hw]
    u_ref[...] = proj[:, 4 * hw:]


def _inproj(x2, ln_g, ln_b, w_in_bf, lb_logits, *, tm):
    n, d = x2.shape
    cols = w_in_bf.shape[1]
    hw = lb_logits.shape[1]
    uw = cols - 4 * hw
    row = lambda w: pl.BlockSpec((tm, w), lambda i: (i, 0))
    full = lambda a: pl.BlockSpec(a.shape, lambda i: (0,) * a.ndim)
    wide = lambda w, dt: jax.ShapeDtypeStruct((n, w), dt)
    outs = [wide(d, F32), wide(hw, F32), wide(hw, F32), wide(hw, F32), wide(hw, BF16),
            wide(hw, F32), wide(uw, F32)]
    return pl.pallas_call(
        functools.partial(_inproj_kernel, hw=hw),
        grid=(n // tm,),
        in_specs=[row(d), full(ln_g), full(ln_b), full(w_in_bf), full(lb_logits)],
        out_specs=[row(d)] + [row(hw)] * 5 + [row(uw)],
        out_shape=outs,
        compiler_params=_params("parallel"),
        name="inproj",
    )(x2, ln_g, ln_b, w_in_bf, lb_logits)


def _hgrn_masks():
    t = np.arange(CHUNK)
    le = t[None, :] <= t[:, None]
    same_sub = (t[None, :] // SUB) == (t[:, None] // SUB)
    m = np.concatenate([le, le & same_sub, same_sub], axis=0)
    return m.astype(np.float32)


def _diag_selector():
    s = np.repeat(np.arange(SUB), HG_DK)
    c = np.arange(CHUNK)
    return (s[:, None] == (c[None, :] % SUB)).astype(np.float32)


def _hgrn_kernel(q_ref, g_ref, k_ref, v_ref, gate_ref, ng_ref, m3_ref, sel_ref,
                 o_ref, state_ref, *, tc):
    nch = tc // CHUNK
    nsub = tc // SUB

    @pl.when(pl.program_id(1) == 0)
    def _():
        state_ref[...] = jnp.zeros_like(state_ref)

    g = g_ref[...]
    g_hi = _bf(g)
    r1 = g - g_hi.astype(F32)
    g_mid = _bf(r1)
    g_lo = _bf(r1 - g_mid.astype(F32))
    g_parts = jnp.concatenate([g_hi, g_mid, g_lo], axis=1)
    hw = g.shape[1]
    m3 = m3_ref[...]
    chunk_sums = []
    for c in range(nch):
        r = jnp.dot(m3, g_parts[c * CHUNK:(c + 1) * CHUNK], preferred_element_type=F32)
        chunk_sums.append(r[:, 0:hw] + r[:, hw:2 * hw] + r[:, 2 * hw:])
    part = lambda i: jnp.concatenate(
        [cs_[i * CHUNK:(i + 1) * CHUNK] for cs_ in chunk_sums], axis=0)
    cum_all, cs_all, tot_all = part(0), part(1), part(2)
    row = lax.broadcasted_iota(jnp.int32, (CHUNK, CHUNK), 0)
    col = lax.broadcasted_iota(jnp.int32, (CHUNK, CHUNK), 1)
    blk = row // SUB - col // SUB
    diag_ok = (blk == 0) & (col <= row)
    sel = sel_ref[...]
    ng = ng_ref[...]

    def prepare(h):
        lanes = slice(h * HG_DK, (h + 1) * HG_DK)
        q = q_ref[:, lanes]
        k = k_ref[:, lanes]
        cum = cum_all[:, lanes]
        cs = cs_all[:, lanes]
        tot = tot_all[:, lanes]
        cum3 = cum.reshape(nch, CHUNK, HG_DK)
        last = cum3[:, CHUNK - 1:CHUNK, :]
        q_inter = _bf(q * jnp.exp(cum))
        k_state = _bf((k.reshape(nch, CHUNK, HG_DK) * jnp.exp(last - cum3)).reshape(tc, HG_DK))
        q_sub = q * jnp.exp(cs)
        k_sub = _bf(k * jnp.exp(tot - cs))
        zeros1 = jnp.zeros((SUB, HG_DK), F32)
        tot_p1 = jnp.concatenate([zeros1, tot[:tc - SUB]], axis=0)
        tot_p2 = jnp.concatenate([zeros1, zeros1, tot[:tc - 2 * SUB]], axis=0)
        q_sub2 = q_sub * jnp.exp(tot_p1)
        q_sub3 = q_sub2 * jnp.exp(tot_p2)
        q_subs = (_bf(q_sub), _bf(q_sub2), _bf(q_sub3))
        cs3 = cs.reshape(nsub, SUB, HG_DK)
        q3 = _bf(q).reshape(nsub, SUB, HG_DK)
        k3 = _bf(k).reshape(nsub, SUB, HG_DK)
        pieces = []
        for s in range(SUB):
            decay = jnp.exp(jnp.minimum(cs3 - cs3[:, s:s + 1, :], 0.0))
            pieces.append((q3 * _bf(decay) * k3[:, s:s + 1, :]).reshape(tc, HG_DK))
        diag = jnp.dot(jnp.concatenate(pieces, axis=1), sel,
                       preferred_element_type=F32)
        return q_inter, k_state, q_subs, k_sub, diag, jnp.exp(last)

    def recur(h, prepared):
        q_inter, k_state, q_subs, k_sub, diag, decay_last = prepared
        lanes = slice(h * HG_DK, (h + 1) * HG_DK)
        v_bf = v_ref[:, lanes]
        st = state_ref[h]
        outs = []
        for c in range(nch):
            rows = slice(c * CHUNK, (c + 1) * CHUNK)
            inter = lax.dot_general(q_inter[rows], _bf(st), (((1,), (1,)), ((), ())),
                                    preferred_element_type=F32)
            lhs = jnp.concatenate([qs[rows] for qs in q_subs], axis=0)
            off = lax.dot_general(lhs, k_sub[rows], (((1,), (1,)), ((), ())),
                                  preferred_element_type=F32)
            sc = jnp.where(diag_ok, diag[rows],
                           jnp.where(blk == 1, off[0:CHUNK],
                                     jnp.where(blk == 2, off[CHUNK:2 * CHUNK],
                                               jnp.where(blk == 3, off[2 * CHUNK:], 0.0))))
            intra = jnp.dot(_bf(sc), v_bf[rows], preferred_element_type=F32)
            outs.append(inter + intra)
            upd = lax.dot_general(v_bf[rows], k_state[rows], (((0,), (0,)), ((), ())),
                                  preferred_element_type=F32)
            st = st * decay_last[c] + upd
        state_ref[h] = st
        o = jnp.concatenate(outs, axis=0)
        rms = lax.rsqrt(jnp.mean(o * o, axis=-1, keepdims=True) + RMS_EPS)
        gate = gate_ref[:, lanes]
        o_ref[:, lanes] = _bf(o * rms * ng * (gate * _sigmoid(gate)))

    prepared = prepare(0)
    for h in range(HG_HEADS):
        nxt = prepare(h + 1) if h + 1 < HG_HEADS else None
        recur(h, prepared)
        prepared = nxt


def _hgrn(q, gl, k, v, gate, norm_g, *, batch, tc):
    n, hw = q.shape
    nblk = n // batch // tc
    m3 = jnp.asarray(_hgrn_masks(), dtype=BF16)
    sel = jnp.asarray(_diag_selector(), dtype=BF16)
    row = pl.BlockSpec((tc, hw), lambda b, s: (b * nblk + s, 0))
    full = lambda a: pl.BlockSpec(a.shape, lambda b, s: (0,) * a.ndim)
    return pl.pallas_call(
        functools.partial(_hgrn_kernel, tc=tc),
        grid=(batch, nblk),
        in_specs=[row] * 5 + [full(norm_g), full(m3), full(sel)],
        out_specs=row,
        out_shape=jax.ShapeDtypeStruct((n, hw), BF16),
        scratch_shapes=[pltpu.VMEM((HG_HEADS, HG_DK, HG_DK), F32)],
        compiler_params=_params("parallel", "arbitrary"),
        name="hgrn2",
    )(q, gl, k, v, gate, norm_g, m3, sel)


def _time_major_perm(batch, tt):
    r = np.arange(batch * tt)
    p = np.zeros((batch * tt, batch * tt), np.float32)
    p[r, (r % batch) * tt + r // batch] = 1.0
    return p


def _s5_kernel(u_ref, perm_ref, permt_ref, ar_ref, ai_ref, wb_ref, wc_ref, d_ref, wg_ref, bg_ref,
               o_ref, bu_ref, st_ref, *, tt):
    batch, _, width = u_ref.shape
    ns = ar_ref.shape[1]
    half = ns // 2

    @pl.when(pl.program_id(0) == 0)
    def _():
        st_ref[...] = jnp.zeros_like(st_ref)

    u2 = u_ref[...].reshape(batch * tt, width)
    u_hi = _bf(u2)
    u_lo = _bf(u2 - u_hi.astype(F32))
    perm = perm_ref[...]
    u_hi_tm = jnp.dot(perm, u_hi, preferred_element_type=F32)
    u_tm = u_hi_tm + jnp.dot(perm, u_lo, preferred_element_type=F32)
    cb = 2 * LANES
    nblk = width // cb
    sb = ns // nblk
    for kb in range(nblk):
        ub = _bf(u_hi_tm[:, kb * cb:(kb + 1) * cb])
        for part in range(2):
            cols = slice(part * ns + kb * sb, part * ns + (kb + 1) * sb)
            bu_ref[:, cols] = jnp.dot(ub, wb_ref[kb * cb:(kb + 1) * cb, cols],
                                      preferred_element_type=F32)

    for part in range(2):
        lo = part * half
        ar = jnp.broadcast_to(ar_ref[:, lo:lo + half], (batch, half))
        ai = jnp.broadcast_to(ai_ref[:, lo:lo + half], (batch, half))

        def step(t, carry, lo=lo, ar=ar, ai=ai):
            xr, xi = carry
            rows = pl.ds(pl.multiple_of(t * batch, batch), batch)
            nr = ar * xr - ai * xi + bu_ref[rows, lo:lo + half]
            ni = ar * xi + ai * xr + bu_ref[rows, ns + lo:ns + lo + half]
            bu_ref[rows, lo:lo + half] = nr
            bu_ref[rows, ns + lo:ns + lo + half] = ni
            return nr, ni

        xr, xi = lax.fori_loop(0, tt, step, (st_ref[:, lo:lo + half],
                                             st_ref[:, ns + lo:ns + lo + half]))
        st_ref[:, lo:lo + half] = xr
        st_ref[:, ns + lo:ns + lo + half] = xi

    ys = []
    for ob in range(nblk):
        ocols = slice(ob * cb, (ob + 1) * cb)
        acc = None
        for part in range(2):
            rows = slice(part * ns + ob * sb, part * ns + (ob + 1) * sb)
            t = jnp.dot(_bf(bu_ref[:, rows]), wc_ref[rows, ocols], preferred_element_type=F32)
            acc = t if acc is None else acc + t
        ys.append(acc)
    y = jnp.concatenate(ys, axis=1) + d_ref[...] * u_tm
    z = _gelu_tanh(y)
    gl = jnp.dot(_bf(z), wg_ref[...], preferred_element_type=F32) + bg_ref[...]
    out_tm = _bf(z * _sigmoid(gl))
    out = jnp.dot(permt_ref[...], out_tm, preferred_element_type=F32)
    o_ref[...] = _bf(out).reshape(batch, tt, width)


def _s5(u3, abar_re, abar_im, wb, wc, d, wg, bg, *, tt):
    batch, seq, width = u3.shape
    ns = abar_re.shape[1]
    perm = _time_major_perm(batch, tt)
    perm_bf = jnp.asarray(perm, dtype=BF16)
    permt_bf = jnp.asarray(perm.T, dtype=BF16)
    blk = pl.BlockSpec((batch, tt, width), lambda s: (0, s, 0))
    full = lambda a: pl.BlockSpec(a.shape, lambda s: (0,) * a.ndim)
    consts = (perm_bf, permt_bf, abar_re, abar_im, wb, wc, d, wg, bg)
    return pl.pallas_call(
        functools.partial(_s5_kernel, tt=tt),
        grid=(seq // tt,),
        in_specs=[blk] + [full(a) for a in consts],
        out_specs=blk,
        out_shape=jax.ShapeDtypeStruct(u3.shape, BF16),
        scratch_shapes=[pltpu.VMEM((batch * tt, 2 * ns), F32),
                        pltpu.VMEM((batch, 2 * ns), F32)],
        compiler_params=_params("arbitrary"),
        name="s5",
    )(u3, *consts)


def _s5_discretise(a_re, a_im, log_step, b_re, b_im, c_re, c_im):
    groups, nstate, p = b_re.shape
    step = jnp.exp(log_step.astype(F32))[:, None]
    mag = jnp.exp(step * a_re)
    abar_re = mag * jnp.cos(step * a_im)
    abar_im = mag * jnp.sin(step * a_im)
    num_re, num_im = abar_re - 1.0, abar_im
    den = jnp.square(a_re) + jnp.square(a_im)
    coef_re = ((num_re * a_re + num_im * a_im) / den)[..., None]
    coef_im = ((num_im * a_re - num_re * a_im) / den)[..., None]
    bbar_re = coef_re * b_re - coef_im * b_im
    bbar_im = coef_re * b_im + coef_im * b_re
    eye = jnp.eye(groups, dtype=F32)
    bd = lambda m: jnp.einsum("gnp,gh->gphn", m, eye).reshape(groups * p, groups * nstate)
    wb = jnp.concatenate([bd(bbar_re), bd(bbar_im)], axis=1)
    cd = lambda m: jnp.einsum("gpn,gh->gnhp", m, eye).reshape(groups * nstate, groups * p)
    wc = jnp.concatenate([cd(c_re), -cd(c_im)], axis=0)
    flat = lambda m: m.reshape(1, groups * nstate)
    return flat(abar_re), flat(abar_im), _bf(wb), _bf(wc)


def _oddeven_merge_sort(n):
    pairs = []

    def merge(lo, hi, r):
        step = r * 2
        if step < hi - lo:
            merge(lo, hi, step)
            merge(lo + r, hi, step)
            pairs.extend((i, i + r) for i in range(lo + r, hi - r, step))
        else:
            pairs.append((lo, lo + r))

    def sort(lo, hi):
        if hi - lo >= 1:
            mid = lo + (hi - lo) // 2
            sort(lo, mid)
            sort(mid + 1, hi)
            merge(lo, hi, 1)

    sort(0, n - 1)
    return pairs


def _bitonic_merge(n):
    pairs = []
    dist = n // 2
    while dist >= 1:
        pairs.extend((i, i + dist) for i in range(n) if not i & dist)
        dist //= 2
    return pairs


def _compare_exchange(xs, pairs):
    for i, j in pairs:
        xs[i], xs[j] = jnp.maximum(xs[i], xs[j]), jnp.minimum(xs[i], xs[j])


def _sorted_top(s):
    n = s.shape[0] // SUBLANES
    assert n == PEER_TOPK and n & (n - 1) == 0
    xs = [s[g * SUBLANES:(g + 1) * SUBLANES, :] for g in range(n)]
    _compare_exchange(xs, _oddeven_merge_sort(n))
    shift = SUBLANES // 2
    while shift >= 1:
        rolled = [pltpu.roll(x, shift, 0) for x in xs]
        xs = [jnp.maximum(xs[r], rolled[n - 1 - r]) for r in range(n)]
        _compare_exchange(xs, _bitonic_merge(n))
        shift //= 2
    return [x[0:1, :] for x in xs]


def _rank_among(vals, s):
    rank = jnp.full(s.shape, float(len(vals)), F32)
    for b in range(len(vals) - 1, -1, -1):
        rank = jnp.where(vals[b] > s, rank, float(b))
    return rank


def _stack_rows(rows, nrows):
    t = rows[0].shape[1]
    idx = lax.broadcasted_iota(jnp.int32, (nrows, t), 0)
    out = jnp.zeros((nrows, t), F32)
    for r, v in enumerate(rows):
        out = jnp.where(idx == r, v, out)
    return out


def _route_head(s1, s2):
    v1 = _sorted_top(s1)
    v2 = _sorted_top(s2)
    rank2 = _rank_among(v2, s2)
    t = s1.shape[1]
    v2_lo = _stack_rows(v2[:8], 8)
    v2_hi = _stack_rows(v2[8:], 8)
    v1_hi = _stack_rows(v1[8:], 8)
    idx8 = lax.broadcasted_iota(jnp.int32, (8, t), 0)
    cand = [v1[0] + v2_lo, v1[0] + v2_hi]
    for a in range(1, 8):
        nb = PEER_TOPK // (a + 1)
        c = v1[a] + v2_lo
        cand.append(c if nb >= 8 else jnp.where(idx8 < nb, c, NEG_INF))
    cand.append(v1_hi + v2[0])
    cur = jnp.concatenate(cand, axis=0)
    top = v1[0] + v2[0]
    z = jnp.zeros_like(top)
    tau = top
    for _ in range(PEER_TOPK):
        tau = jnp.max(cur, axis=0, keepdims=True)
        z = z + jnp.exp(tau - top)
        cur = jnp.where(cur == tau, NEG_INF, cur)
    b0 = 4
    cnt = jnp.zeros(s1.shape, F32)
    for b in range(b0):
        cnt = cnt + jnp.where(s1 + v2[b] >= tau, 1.0, 0.0)
    for a in range(PEER_TOPK // (b0 + 1)):
        extra = jnp.zeros_like(tau)
        for b in range(b0, PEER_TOPK):
            extra = extra + jnp.where(v1[a] + v2[b] >= tau, 1.0, 0.0)
        cnt = cnt + jnp.where(s1 == v1[a], extra, 0.0)
    e1 = jnp.exp(s1 - v1[0])
    e2 = jnp.exp(s2 - v2[0]) / z
    return e1, cnt, e2, rank2


def _route_kernel(ma_ref, mb_ref, h0_ref, wo_ref, g_ref, b_ref, wq_ref, k1_ref, k2_ref,
                  h1t_ref, h1tb_ref, e1_ref, cn_ref, e2_ref, r2_ref):
    hw = ma_ref.shape[1]
    mixed = jnp.dot(ma_ref[...], wo_ref[0:hw, :], preferred_element_type=F32)
    mixed = mixed + jnp.dot(mb_ref[...], wo_ref[hw:, :], preferred_element_type=F32)
    h1 = _layer_norm(DN_ALPHA * h0_ref[...] + mixed, g_ref[...], b_ref[...])
    h1t = h1.T
    h1t_ref[...] = h1t
    h1tb = _bf(h1t)
    h1tb_ref[...] = h1tb
    qt = jnp.dot(wq_ref[...], h1tb, preferred_element_type=F32)
    k1 = k1_ref[...]
    k2 = k2_ref[...]
    for h in range(PEER_HEADS):
        base = h * 2 * PEER_HALF
        s1 = jnp.dot(k1, _bf(qt[base:base + PEER_HALF]), preferred_element_type=F32)
        s2 = jnp.dot(k2, _bf(qt[base + PEER_HALF:base + 2 * PEER_HALF]),
                     preferred_element_type=F32)
        e1, cnt, e2, rank2 = _route_head(s1, s2)
        e1_ref[h] = e1
        cn_ref[h] = cnt
        e2_ref[h] = _bf(e2)
        r2_ref[h] = _bf(rank2)


def _route(mix_a, mix_b, h0, w_out_bf, ln_g, ln_b, wq_t_bf, k1_bf, k2_bf, *, tm):
    n, d = h0.shape
    hw = mix_a.shape[1]
    row = lambda w: pl.BlockSpec((tm, w), lambda i: (i, 0))
    full = lambda a: pl.BlockSpec(a.shape, lambda i: (0,) * a.ndim)
    colblk = pl.BlockSpec((d, tm), lambda i: (0, i))
    tab = pl.BlockSpec((PEER_HEADS, PEER_KEYS, tm), lambda i: (0, 0, i))
    tab_shape = (PEER_HEADS, PEER_KEYS, n)
    return pl.pallas_call(
        _route_kernel,
        grid=(n // tm,),
        in_specs=[row(hw), row(hw), row(d)] +
                 [full(a) for a in (w_out_bf, ln_g, ln_b, wq_t_bf, k1_bf, k2_bf)],
        out_specs=[colblk, colblk, tab, tab, tab, tab],
        out_shape=[jax.ShapeDtypeStruct((d, n), F32), jax.ShapeDtypeStruct((d, n), BF16),
                   jax.ShapeDtypeStruct(tab_shape, F32), jax.ShapeDtypeStruct(tab_shape, F32),
                   jax.ShapeDtypeStruct(tab_shape, BF16), jax.ShapeDtypeStruct(tab_shape, BF16)],
        compiler_params=_params("parallel"),
        name="route",
    )(mix_a, mix_b, h0, w_out_bf, ln_g, ln_b, wq_t_bf, k1_bf, k2_bf)


def _experts_kernel(wd0_ref, wd_ref, wut_ref, h1tb_ref, h1t_ref, e1_ref, cn_ref, e2_ref, r2_ref,
                    g_ref, b_ref, o_ref, acc_ref, a0_ref, a1_ref, p_ref, *, ib, nk):
    s = pl.program_id(1)
    tt = h1tb_ref.shape[1]
    half = tt // 2

    @pl.when(s == 0)
    def _():
        acc_ref[...] = jnp.zeros_like(acc_ref)
        a0_ref[...] = jnp.dot(wd0_ref[...], h1tb_ref[...], preferred_element_type=F32)

    def step(a_cur, a_nxt):
        for hf in range(tt // half):
            hl = pl.ds(hf * half, half)
            for ii in range(ib):
                gsum = jnp.zeros((PEER_KEYS, half), BF16)
                for h in range(PEER_HEADS):
                    e1 = _bf(jnp.broadcast_to(e1_ref[h, ii:ii + 1, hl], gsum.shape))
                    cn = _bf(jnp.broadcast_to(cn_ref[h, ii:ii + 1, hl], gsum.shape))
                    val = e2_ref[h, :, hl] * e1
                    gsum = gsum + jnp.where(r2_ref[h, :, hl] < cn, val, jnp.zeros_like(val))
                rows = pl.ds(ii * PEER_KEYS, PEER_KEYS)
                p_ref[rows, hl] = _bf(_gelu(a_cur[rows, hl])) * gsum
            a_nxt[:, hl] = jnp.dot(wd_ref[...], h1tb_ref[:, hl], preferred_element_type=F32)
            acc_ref[:, hl] += jnp.dot(wut_ref[...], p_ref[:, hl], preferred_element_type=F32)

    @pl.when(s % 2 == 0)
    def _():
        step(a0_ref, a1_ref)

    @pl.when(s % 2 == 1)
    def _():
        step(a1_ref, a0_ref)

    @pl.when(s == nk - 1)
    def _():
        y = DN_ALPHA * h1t_ref[...] + acc_ref[...]
        mu = jnp.mean(y, axis=0, keepdims=True)
        yc = y - mu
        var = jnp.mean(yc * yc, axis=0, keepdims=True)
        yn = yc * lax.rsqrt(var + LN_EPS)
        o_ref[...] = yn.T * g_ref[...] + b_ref[...]


def _experts(wd_bf, wu_bf, h1tb, h1t, e1, cn, e2, r2, ln_g, ln_b, *, tt, ib):
    d, n = h1t.shape
    nexp = wd_bf.shape[0]
    eb = ib * PEER_KEYS
    nk = nexp // eb
    wut_bf = wu_bf.reshape(nk, eb, d).transpose(0, 2, 1)
    last = nk - 1
    tokcol = pl.BlockSpec((d, tt), lambda j, s: (0, j))
    tab_i = pl.BlockSpec((PEER_HEADS, ib, tt), lambda j, s: (0, s, j))
    tab_full = pl.BlockSpec((PEER_HEADS, PEER_KEYS, tt), lambda j, s: (0, 0, j))
    full = lambda a: pl.BlockSpec(a.shape, lambda j, s: (0,) * a.ndim)
    return pl.pallas_call(
        functools.partial(_experts_kernel, ib=ib, nk=nk),
        grid=(n // tt, nk),
        in_specs=[pl.BlockSpec((eb, d), lambda j, s: (0, 0)),
                  pl.BlockSpec((eb, d), lambda j, s: (jnp.minimum(s + 1, last), 0)),
                  pl.BlockSpec((None, d, eb), lambda j, s: (s, 0, 0)),
                  tokcol, tokcol, tab_i, tab_i, tab_full, tab_full, full(ln_g), full(ln_b)],
        out_specs=pl.BlockSpec((tt, d), lambda j, s: (j, 0)),
        out_shape=jax.ShapeDtypeStruct((n, d), F32),
        scratch_shapes=[pltpu.VMEM((d, tt), F32),
                        pltpu.VMEM((eb, tt), F32), pltpu.VMEM((eb, tt), F32),
                        pltpu.VMEM((eb, tt), BF16)],
        compiler_params=_params("parallel", "arbitrary"),
        name="experts",
    )(wd_bf, wd_bf, wut_bf, h1tb, h1t, e1, cn, e2, r2, ln_g, ln_b)


class _Tiles(NamedTuple):
    inproj_rows: int
    hgrn_steps: int
    s5_steps: int
    route_tokens: int
    expert_tokens: int
    expert_first_keys: int


def _tiles(n, seq):
    pick = lambda size, preferred: preferred if size % preferred == 0 else size
    return _Tiles(inproj_rows=pick(n, 512), hgrn_steps=pick(seq, 512), s5_steps=pick(seq, 64),
                  route_tokens=pick(n, 256), expert_tokens=pick(n, 512), expert_first_keys=8)


def kernel(x, ln0_g, ln0_b, w_in, hg_lb_logits, hg_norm_g, s5_a_re, s5_a_im, s5_log_step,
           s5_b_re, s5_b_im, s5_c_re, s5_c_im, s5_d, w_glu, b_glu, w_out, ln1_g, ln1_b,
           w_query, peer_keys_1, peer_keys_2, peer_down, peer_up, ln2_g, ln2_b):
    assert w_in.shape[0] == 1, "single-layer trunk"
    batch, seq, d = x.shape
    n = batch * seq
    r2d = lambda a: a.reshape(1, -1).astype(F32)
    tiles = _tiles(n, seq)

    x2 = x.reshape(n, d).astype(F32)
    h0, q, gl, k, v, gate, u = _inproj(
        x2, r2d(ln0_g), r2d(ln0_b), _bf(w_in[0]), hg_lb_logits.astype(F32),
        tm=tiles.inproj_rows)

    mix_a = _hgrn(q, gl, k, v, gate, r2d(hg_norm_g[0]), batch=batch,
                  tc=tiles.hgrn_steps)

    abar_re, abar_im, wb, wc = _s5_discretise(
        s5_a_re[0].astype(F32), s5_a_im[0].astype(F32), s5_log_step[0],
        s5_b_re[0].astype(F32), s5_b_im[0].astype(F32), s5_c_re[0], s5_c_im[0])
    mix_b = _s5(u.reshape(batch, seq, -1), abar_re, abar_im, wb, wc, r2d(s5_d[0]),
                _bf(w_glu[0]), r2d(b_glu[0]), tt=tiles.s5_steps).reshape(n, -1)

    h1t, h1tb, e1, cn, e2, r2 = _route(
        mix_a, mix_b, h0, _bf(w_out[0]), r2d(ln1_g[0]), r2d(ln1_b[0]),
        _bf(w_query[0].T), _bf(peer_keys_1[0]), _bf(peer_keys_2[0]), tm=tiles.route_tokens)

    out = _experts(_bf(peer_down[0]), _bf(peer_up[0]), h1tb, h1t, e1, cn, e2, r2,
                   r2d(ln2_g[0]), r2d(ln2_b[0]), tt=tiles.expert_tokens,
                   ib=tiles.expert_first_keys)
    return out.reshape(batch, seq, d).astype(x.dtype)
```

```python
import functools
import math
from typing import NamedTuple

import jax
import jax.numpy as jnp
import numpy as np
from jax import lax
from jax.experimental import pallas as pl
from jax.experimental.pallas import tpu as pltpu

F32 = jnp.float32
BF16 = jnp.bfloat16

CHUNK = 64
SUB = 16
HG_DK = 128
HG_HEADS = 4
PEER_KEYS = 128
PEER_HEADS = 8
PEER_TOPK = 16
PEER_HALF = 128
LN_EPS = 1e-5
RMS_EPS = 1e-6
DN_ALPHA = 2.0 ** 0.25
NEG_INF = float("-inf")
LANES = 128
SUBLANES = 8

VMEM_LIMIT_BYTES = 56 * 1024 * 1024


def _params(*semantics):
    return pltpu.CompilerParams(dimension_semantics=semantics,
                                vmem_limit_bytes=VMEM_LIMIT_BYTES)


def _bf(x):
    return x.astype(BF16)


def _layer_norm(x, g, b):
    mu = jnp.mean(x, axis=-1, keepdims=True)
    xc = x - mu
    var = jnp.mean(xc * xc, axis=-1, keepdims=True)
    return xc * lax.rsqrt(var + LN_EPS) * g + b


def _sigmoid(x):
    return 1.0 / (1.0 + jnp.exp(-x))


def _gelu_tanh(x):
    c = math.sqrt(2.0 / math.pi)
    return x * (0.5 * (1.0 + jnp.tanh(c * (x + 0.044715 * (x * x * x)))))


def _gelu(x):
    k0 = -2.0 * math.sqrt(2.0 / math.pi) * math.log2(math.e)
    w = x * (x * x * (k0 * 0.044715) + k0)
    return x / (1.0 + jnp.exp2(w))


def _inproj_kernel(x_ref, g_ref, b_ref, w_ref, lbl_ref,
                   h_ref, q_ref, gl_ref, k_ref, v_ref, gate_ref, u_ref, *, hw):
    h = _layer_norm(x_ref[...], g_ref[...], b_ref[...])
    h_ref[...] = h
    proj = jnp.dot(_bf(h), w_ref[...], preferred_element_type=F32)
    logits = lbl_ref[...]
    e = jnp.exp(logits - jnp.max(logits, axis=0, keepdims=True))
    lb = e[0:1, :] / jnp.sum(e, axis=0, keepdims=True)
    forget = lb + (1.0 - lb) * _sigmoid(proj[:, hw:2 * hw])
    q_ref[...] = proj[:, 0:hw] * (HG_DK ** -0.5)
    gl_ref[...] = jnp.log(forget)
    k_ref[...] = 1.0 - forget
    v_ref[...] = _bf(proj[:, 2 * hw:3 * hw])
    gate_ref[...] = proj[:, 3 * hw:4 * hw]
    u_ref[...] = proj[:, 4 * hw:]


def _inproj(x2, ln_g, ln_b, w_in_bf, lb_logits, *, tm):
    n, d = x2.shape
    cols = w_in_bf.shape[1]
    hw = lb_logits.shape[1]
    uw = cols - 4 * hw
    row = lambda w: pl.BlockSpec((tm, w), lambda i: (i, 0))
    full = lambda a: pl.BlockSpec(a.shape, lambda i: (0,) * a.ndim)
    wide = lambda w, dt: jax.ShapeDtypeStruct((n, w), dt)
    outs = [wide(d, F32), wide(hw, F32), wide(hw, F32), wide(hw, F32), wide(hw, BF16),
            wide(hw, F32), wide(uw, F32)]
    return pl.pallas_call(
        functools.partial(_inproj_kernel, hw=hw),
        grid=(n // tm,),
        in_specs=[row(d), full(ln_g), full(ln_b), full(w_in_bf), full(lb_logits)],
        out_specs=[row(d)] + [row(hw)] * 5 + [row(uw)],
        out_shape=outs,
        compiler_params=_params("parallel"),
        name="inproj",
    )(x2, ln_g, ln_b, w_in_bf, lb_logits)


def _hgrn_masks():
    t = np.arange(CHUNK)
    le = t[None, :] <= t[:, None]
    same_sub = (t[None, :] // SUB) == (t[:, None] // SUB)
    m = np.concatenate([le, le & same_sub, same_sub], axis=0)
    return m.astype(np.float32)


def _diag_selector():
    s = np.repeat(np.arange(SUB), HG_DK)
    c = np.arange(CHUNK)
    return (s[:, None] == (c[None, :] % SUB)).astype(np.float32)


def _hgrn_kernel(q_ref, g_ref, k_ref, v_ref, gate_ref, ng_ref, m3_ref, sel_ref,
                 o_ref, state_ref, *, tc):
    nch = tc // CHUNK
    nsub = tc // SUB

    @pl.when(pl.program_id(1) == 0)
    def _():
        state_ref[...] = jnp.zeros_like(state_ref)

    g = g_ref[...]
    g_hi = _bf(g)
    r1 = g - g_hi.astype(F32)
    g_mid = _bf(r1)
    g_lo = _bf(r1 - g_mid.astype(F32))
    g_parts = jnp.concatenate([g_hi, g_mid, g_lo], axis=1)
    hw = g.shape[1]
    m3 = m3_ref[...]
    chunk_sums = []
    for c in range(nch):
        r = jnp.dot(m3, g_parts[c * CHUNK:(c + 1) * CHUNK], preferred_element_type=F32)
        chunk_sums.append(r[:, 0:hw] + r[:, hw:2 * hw] + r[:, 2 * hw:])
    part = lambda i: jnp.concatenate(
        [cs_[i * CHUNK:(i + 1) * CHUNK] for cs_ in chunk_sums], axis=0)
    cum_all, cs_all, tot_all = part(0), part(1), part(2)
    row = lax.broadcasted_iota(jnp.int32, (CHUNK, CHUNK), 0)
    col = lax.broadcasted_iota(jnp.int32, (CHUNK, CHUNK), 1)
    blk = row // SUB - col // SUB
    diag_ok = (blk == 0) & (col <= row)
    sel = sel_ref[...]
    ng = ng_ref[...]

    def prepare(h):
        lanes = slice(h * HG_DK, (h + 1) * HG_DK)
        q = q_ref[:, lanes]
        k = k_ref[:, lanes]
        cum = cum_all[:, lanes]
        cs = cs_all[:, lanes]
        tot = tot_all[:, lanes]
        cum3 = cum.reshape(nch, CHUNK, HG_DK)
        last = cum3[:, CHUNK - 1:CHUNK, :]
        q_inter = _bf(q * jnp.exp(cum))
        k_state = _bf((k.reshape(nch, CHUNK, HG_DK) * jnp.exp(last - cum3)).reshape(tc, HG_DK))
        q_sub = q * jnp.exp(cs)
        k_sub = _bf(k * jnp.exp(tot - cs))
        zeros1 = jnp.zeros((SUB, HG_DK), F32)
        tot_p1 = jnp.concatenate([zeros1, tot[:tc - SUB]], axis=0)
        tot_p2 = jnp.concatenate([zeros1, zeros1, tot[:tc - 2 * SUB]], axis=0)
        q_sub2 = q_sub * jnp.exp(tot_p1)
        q_sub3 = q_sub2 * jnp.exp(tot_p2)
        q_subs = (_bf(q_sub), _bf(q_sub2), _bf(q_sub3))
        cs3 = cs.reshape(nsub, SUB, HG_DK)
        q3 = _bf(q).reshape(nsub, SUB, HG_DK)
        k3 = _bf(k).reshape(nsub, SUB, HG_DK)
        pieces = []
        for s in range(SUB):
            decay = jnp.exp(jnp.minimum(cs3 - cs3[:, s:s + 1, :], 0.0))
            pieces.append((q3 * _bf(decay) * k3[:, s:s + 1, :]).reshape(tc, HG_DK))
        diag = jnp.dot(jnp.concatenate(pieces, axis=1), sel,
                       preferred_element_type=F32)
        return q_inter, k_state, q_subs, k_sub, diag, jnp.exp(last)

    def recur(h, prepared):
        q_inter, k_state, q_subs, k_sub, diag, decay_last = prepared
        lanes = slice(h * HG_DK, (h + 1) * HG_DK)
        v_bf = v_ref[:, lanes]
        st = state_ref[h]
        outs = []
        for c in range(nch):
            rows = slice(c * CHUNK, (c + 1) * CHUNK)
            inter = lax.dot_general(q_inter[rows], _bf(st), (((1,), (1,)), ((), ())),
                                    preferred_element_type=F32)
            lhs = jnp.concatenate([qs[rows] for qs in q_subs], axis=0)
            off = lax.dot_general(lhs, k_sub[rows], (((1,), (1,)), ((), ())),
                                  preferred_element_type=F32)
            sc = jnp.where(diag_ok, diag[rows],
                           jnp.where(blk == 1, off[0:CHUNK],
                                     jnp.where(blk == 2, off[CHUNK:2 * CHUNK],
                                               jnp.where(blk == 3, off[2 * CHUNK:], 0.0))))
            intra = jnp.dot(_bf(sc), v_bf[rows], preferred_element_type=F32)
            outs.append(inter + intra)
            upd = lax.dot_general(v_bf[rows], k_state[rows], (((0,), (0,)), ((), ())),
                                  preferred_element_type=F32)
            st = st * decay_last[c] + upd
        state_ref[h] = st
        o = jnp.concatenate(outs, axis=0)
        rms = lax.rsqrt(jnp.mean(o * o, axis=-1, keepdims=True) + RMS_EPS)
        gate = gate_ref[:, lanes]
        o_ref[:, lanes] = _bf(o * rms * ng * (gate * _sigmoid(gate)))

    prepared = prepare(0)
    for h in range(HG_HEADS):
        nxt = prepare(h + 1) if h + 1 < HG_HEADS else None
        recur(h, prepared)
        prepared = nxt


def _hgrn(q, gl, k, v, gate, norm_g, *, batch, tc):
    n, hw = q.shape
    nblk = n // batch // tc
    m3 = jnp.asarray(_hgrn_masks(), dtype=BF16)
    sel = jnp.asarray(_diag_selector(), dtype=BF16)
    row = pl.BlockSpec((tc, hw), lambda b, s: (b * nblk + s, 0))
    full = lambda a: pl.BlockSpec(a.shape, lambda b, s: (0,) * a.ndim)
    return pl.pallas_call(
        functools.partial(_hgrn_kernel, tc=tc),
        grid=(batch, nblk),
        in_specs=[row] * 5 + [full(norm_g), full(m3), full(sel)],
        out_specs=row,
        out_shape=jax.ShapeDtypeStruct((n, hw), BF16),
        scratch_shapes=[pltpu.VMEM((HG_HEADS, HG_DK, HG_DK), F32)],
        compiler_params=_params("parallel", "arbitrary"),
        name="hgrn2",
    )(q, gl, k, v, gate, norm_g, m3, sel)


def _time_major_perm(batch, tt):
    r = np.arange(batch * tt)
    p = np.zeros((batch * tt, batch * tt), np.float32)
    p[r, (r % batch) * tt + r // batch] = 1.0
    return p


def _s5_kernel(u_ref, perm_ref, permt_ref, ar_ref, ai_ref, wb_ref, wc_ref, d_ref, wg_ref, bg_ref,
               o_ref, bu_ref, st_ref, *, tt):
    batch, _, width = u_ref.shape
    ns = ar_ref.shape[1]
    half = ns // 2

    @pl.when(pl.program_id(0) == 0)
    def _():
        st_ref[...] = jnp.zeros_like(st_ref)

    u2 = u_ref[...].reshape(batch * tt, width)
    u_hi = _bf(u2)
    u_lo = _bf(u2 - u_hi.astype(F32))
    perm = perm_ref[...]
    u_hi_tm = jnp.dot(perm, u_hi, preferred_element_type=F32)
    u_tm = u_hi_tm + jnp.dot(perm, u_lo, preferred_element_type=F32)
    cb = 2 * LANES
    nblk = width // cb
    sb = ns // nblk
    for kb in range(nblk):
        ub = _bf(u_hi_tm[:, kb * cb:(kb + 1) * cb])
        for part in range(2):
            cols = slice(part * ns + kb * sb, part * ns + (kb + 1) * sb)
            bu_ref[:, cols] = jnp.dot(ub, wb_ref[kb * cb:(kb + 1) * cb, cols],
                                      preferred_element_type=F32)

    for part in range(2):
        lo = part * half
        ar = jnp.broadcast_to(ar_ref[:, lo:lo + half], (batch, half))
        ai = jnp.broadcast_to(ai_ref[:, lo:lo + half], (batch, half))

        def step(t, carry, lo=lo, ar=ar, ai=ai):
            xr, xi = carry
            rows = pl.ds(pl.multiple_of(t * batch, batch), batch)
            nr = ar * xr - ai * xi + bu_ref[rows, lo:lo + half]
            ni = ar * xi + ai * xr + bu_ref[rows, ns + lo:ns + lo + half]
            bu_ref[rows, lo:lo + half] = nr
            bu_ref[rows, ns + lo:ns + lo + half] = ni
            return nr, ni

        xr, xi = lax.fori_loop(0, tt, step, (st_ref[:, lo:lo + half],
                                             st_ref[:, ns + lo:ns + lo + half]))
        st_ref[:, lo:lo + half] = xr
        st_ref[:, ns + lo:ns + lo + half] = xi

    ys = []
    for ob in range(nblk):
        ocols = slice(ob * cb, (ob + 1) * cb)
        acc = None
        for part in range(2):
            rows = slice(part * ns + ob * sb, part * ns + (ob + 1) * sb)
            t = jnp.dot(_bf(bu_ref[:, rows]), wc_ref[rows, ocols], preferred_element_type=F32)
            acc = t if acc is None else acc + t
        ys.append(acc)
    y = jnp.concatenate(ys, axis=1) + d_ref[...] * u_tm
    z = _gelu_tanh(y)
    gl = jnp.dot(_bf(z), wg_ref[...], preferred_element_type=F32) + bg_ref[...]
    out_tm = _bf(z * _sigmoid(gl))
    out = jnp.dot(permt_ref[...], out_tm, preferred_element_type=F32)
    o_ref[...] = _bf(out).reshape(batch, tt, width)


def _s5(u3, abar_re, abar_im, wb, wc, d, wg, bg, *, tt):
    batch, seq, width = u3.shape
    ns = abar_re.shape[1]
    perm = _time_major_perm(batch, tt)
    perm_bf = jnp.asarray(perm, dtype=BF16)
    permt_bf = jnp.asarray(perm.T, dtype=BF16)
    blk = pl.BlockSpec((batch, tt, width), lambda s: (0, s, 0))
    full = lambda a: pl.BlockSpec(a.shape, lambda s: (0,) * a.ndim)
    consts = (perm_bf, permt_bf, abar_re, abar_im, wb, wc, d, wg, bg)
    return pl.pallas_call(
        functools.partial(_s5_kernel, tt=tt),
        grid=(seq // tt,),
        in_specs=[blk] + [full(a) for a in consts],
        out_specs=blk,
        out_shape=jax.ShapeDtypeStruct(u3.shape, BF16),
        scratch_shapes=[pltpu.VMEM((batch * tt, 2 * ns), F32),
                        pltpu.VMEM((batch, 2 * ns), F32)],
        compiler_params=_params("arbitrary"),
        name="s5",
    )(u3, *consts)


def _s5_discretise(a_re, a_im, log_step, b_re, b_im, c_re, c_im):
    groups, nstate, p = b_re.shape
    step = jnp.exp(log_step.astype(F32))[:, None]
    mag = jnp.exp(step * a_re)
    abar_re = mag * jnp.cos(step * a_im)
    abar_im = mag * jnp.sin(step * a_im)
    num_re, num_im = abar_re - 1.0, abar_im
    den = jnp.square(a_re) + jnp.square(a_im)
    coef_re = ((num_re * a_re + num_im * a_im) / den)[..., None]
    coef_im = ((num_im * a_re - num_re * a_im) / den)[..., None]
    bbar_re = coef_re * b_re - coef_im * b_im
    bbar_im = coef_re * b_im + coef_im * b_re
    eye = jnp.eye(groups, dtype=F32)
    bd = lambda m: jnp.einsum("gnp,gh->gphn", m, eye).reshape(groups * p, groups * nstate)
    wb = jnp.concatenate([bd(bbar_re), bd(bbar_im)], axis=1)
    cd = lambda m: jnp.einsum("gpn,gh->gnhp", m, eye).reshape(groups * nstate, groups * p)
    wc = jnp.concatenate([cd(c_re), -cd(c_im)], axis=0)
    flat = lambda m: m.reshape(1, groups * nstate)
    return flat(abar_re), flat(abar_im), _bf(wb), _bf(wc)


def _oddeven_merge_sort(n):
    pairs = []

    def merge(lo, hi, r):
        step = r * 2
        if step < hi - lo:
            merge(lo, hi, step)
            merge(lo + r, hi, step)
            pairs.extend((i, i + r) for i in range(lo + r, hi - r, step))
        else:
            pairs.append((lo, lo + r))

    def sort(lo, hi):
        if hi - lo >= 1:
            mid = lo + (hi - lo) // 2
            sort(lo, mid)
            sort(mid + 1, hi)
            merge(lo, hi, 1)

    sort(0, n - 1)
    return pairs


def _bitonic_merge(n):
    pairs = []
    dist = n // 2
    while dist >= 1:
        pairs.extend((i, i + dist) for i in range(n) if not i & dist)
        dist //= 2
    return pairs


def _compare_exchange(xs, pairs):
    for i, j in pairs:
        xs[i], xs[j] = jnp.maximum(xs[i], xs[j]), jnp.minimum(xs[i], xs[j])


def _sorted_top(s):
    n = s.shape[0] // SUBLANES
    assert n == PEER_TOPK and n & (n - 1) == 0
    xs = [s[g * SUBLANES:(g + 1) * SUBLANES, :] for g in range(n)]
    _compare_exchange(xs, _oddeven_merge_sort(n))
    shift = SUBLANES // 2
    while shift >= 1:
        rolled = [pltpu.roll(x, shift, 0) for x in xs]
        xs = [jnp.maximum(xs[r], rolled[n - 1 - r]) for r in range(n)]
        _compare_exchange(xs, _bitonic_merge(n))
        shift //= 2
    return [x[0:1, :] for x in xs]


def _rank_among(vals, s):
    rank = jnp.full(s.shape, float(len(vals)), F32)
    for b in range(len(vals) - 1, -1, -1):
        rank = jnp.where(vals[b] > s, rank, float(b))
    return rank


def _stack_rows(rows, nrows):
    t = rows[0].shape[1]
    idx = lax.broadcasted_iota(jnp.int32, (nrows, t), 0)
    out = jnp.zeros((nrows, t), F32)
    for r, v in enumerate(rows):
        out = jnp.where(idx == r, v, out)
    return out


def _route_head(s1, s2):
    v1 = _sorted_top(s1)
    v2 = _sorted_top(s2)
    rank2 = _rank_among(v2, s2)
    t = s1.shape[1]
    v2_lo = _stack_rows(v2[:8], 8)
    v2_hi = _stack_rows(v2[8:], 8)
    v1_hi = _stack_rows(v1[8:], 8)
    idx8 = lax.broadcasted_iota(jnp.int32, (8, t), 0)
    cand = [v1[0] + v2_lo, v1[0] + v2_hi]
    for a in range(1, 8):
        nb = PEER_TOPK // (a + 1)
        c = v1[a] + v2_lo
        cand.append(c if nb >= 8 else jnp.where(idx8 < nb, c, NEG_INF))
    cand.append(v1_hi + v2[0])
    cur = jnp.concatenate(cand, axis=0)
    top = v1[0] + v2[0]
    z = jnp.zeros_like(top)
    tau = top
    for _ in range(PEER_TOPK):
        tau = jnp.max(cur, axis=0, keepdims=True)
        z = z + jnp.exp(tau - top)
        cur = jnp.where(cur == tau, NEG_INF, cur)
    b0 = 4
    cnt = jnp.zeros(s1.shape, F32)
    for b in range(b0):
        cnt = cnt + jnp.where(s1 + v2[b] >= tau, 1.0, 0.0)
    for a in range(PEER_TOPK // (b0 + 1)):
        extra = jnp.zeros_like(tau)
        for b in range(b0, PEER_TOPK):
            extra = extra + jnp.where(v1[a] + v2[b] >= tau, 1.0, 0.0)
        cnt = cnt + jnp.where(s1 == v1[a], extra, 0.0)
    e1 = jnp.exp(s1 - v1[0])
    e2 = jnp.exp(s2 - v2[0]) / z
    return e1, cnt, e2, rank2


def _route_kernel(ma_ref, mb_ref, h0_ref, wo_ref, g_ref, b_ref, wq_ref, k1_ref, k2_ref,
                  h1t_ref, h1tb_ref, e1_ref, cn_ref, e2_ref, r2_ref):
    hw = ma_ref.shape[1]
    mixed = jnp.dot(ma_ref[...], wo_ref[0:hw, :], preferred_element_type=F32)
    mixed = mixed + jnp.dot(mb_ref[...], wo_ref[hw:, :], preferred_element_type=F32)
    h1 = _layer_norm(DN_ALPHA * h0_ref[...] + mixed, g_ref[...], b_ref[...])
    h1t = h1.T
    h1t_ref[...] = h1t
    h1tb = _bf(h1t)
    h1tb_ref[...] = h1tb
    qt = jnp.dot(wq_ref[...], h1tb, preferred_element_type=F32)
    k1 = k1_ref[...]
    k2 = k2_ref[...]
    for h in range(PEER_HEADS):
        base = h * 2 * PEER_HALF
        s1 = jnp.dot(k1, _bf(qt[base:base + PEER_HALF]), preferred_element_type=F32)
        s2 = jnp.dot(k2, _bf(qt[base + PEER_HALF:base + 2 * PEER_HALF]),
                     preferred_element_type=F32)
        e1, cnt, e2, rank2 = _route_head(s1, s2)
        e1_ref[h] = e1
        cn_ref[h] = cnt
        e2_ref[h] = _bf(e2)
        r2_ref[h] = _bf(rank2)


def _route(mix_a, mix_b, h0, w_out_bf, ln_g, ln_b, wq_t_bf, k1_bf, k2_bf, *, tm):
    n, d = h0.shape
    hw = mix_a.shape[1]
    row = lambda w: pl.BlockSpec((tm, w), lambda i: (i, 0))
    full = lambda a: pl.BlockSpec(a.shape, lambda i: (0,) * a.ndim)
    colblk = pl.BlockSpec((d, tm), lambda i: (0, i))
    tab = pl.BlockSpec((PEER_HEADS, PEER_KEYS, tm), lambda i: (0, 0, i))
    tab_shape = (PEER_HEADS, PEER_KEYS, n)
    return pl.pallas_call(
        _route_kernel,
        grid=(n // tm,),
        in_specs=[row(hw), row(hw), row(d)] +
                 [full(a) for a in (w_out_bf, ln_g, ln_b, wq_t_bf, k1_bf, k2_bf)],
        out_specs=[colblk, colblk, tab, tab, tab, tab],
        out_shape=[jax.ShapeDtypeStruct((d, n), F32), jax.ShapeDtypeStruct((d, n), BF16),
                   jax.ShapeDtypeStruct(tab_shape, F32), jax.ShapeDtypeStruct(tab_shape, F32),
                   jax.ShapeDtypeStruct(tab_shape, BF16), jax.ShapeDtypeStruct(tab_shape, BF16)],
        compiler_params=_params("parallel"),
        name="route",
    )(mix_a, mix_b, h0, w_out_bf, ln_g, ln_b, wq_t_bf, k1_bf, k2_bf)


def _experts_kernel(wd0_ref, wd_ref, wut_ref, h1tbn_ref, h1t_ref, e1_ref, cn_ref, e2_ref, r2_ref,
                    g_ref, b_ref, o_ref, acc_ref, a0_ref, a1_ref, p_ref, *, ib, nk):
    j = pl.program_id(0)
    s = pl.program_id(1)
    tt = h1tbn_ref.shape[1]
    half = tt // 2

    @pl.when(s == 0)
    def _():
        acc_ref[...] = jnp.zeros_like(acc_ref)

    @pl.when((j == 0) & (s == 0))
    def _():
        a0_ref[...] = jnp.dot(wd0_ref[...], h1tbn_ref[...], preferred_element_type=F32)

    def step(a_cur, a_nxt):
        for hf in range(tt // half):
            hl = pl.ds(hf * half, half)
            for ii in range(ib):
                gsum = jnp.zeros((PEER_KEYS, half), BF16)
                for h in range(PEER_HEADS):
                    e1 = _bf(jnp.broadcast_to(e1_ref[h, ii:ii + 1, hl], gsum.shape))
                    cn = _bf(jnp.broadcast_to(cn_ref[h, ii:ii + 1, hl], gsum.shape))
                    val = e2_ref[h, :, hl] * e1
                    gsum = gsum + jnp.where(r2_ref[h, :, hl] < cn, val, jnp.zeros_like(val))
                rows = pl.ds(ii * PEER_KEYS, PEER_KEYS)
                p_ref[rows, hl] = _bf(_gelu(a_cur[rows, hl])) * gsum
            a_nxt[:, hl] = jnp.dot(wd_ref[...], h1tbn_ref[:, hl], preferred_element_type=F32)
            acc_ref[:, hl] += jnp.dot(wut_ref[...], p_ref[:, hl], preferred_element_type=F32)

    @pl.when(s % 2 == 0)
    def _():
        step(a0_ref, a1_ref)

    @pl.when(s % 2 == 1)
    def _():
        step(a1_ref, a0_ref)

    @pl.when(s == nk - 1)
    def _():
        y = DN_ALPHA * h1t_ref[...] + acc_ref[...]
        mu = jnp.mean(y, axis=0, keepdims=True)
        yc = y - mu
        var = jnp.mean(yc * yc, axis=0, keepdims=True)
        yn = yc * lax.rsqrt(var + LN_EPS)
        o_ref[...] = yn.T * g_ref[...] + b_ref[...]


def _experts(wd_bf, wu_bf, h1tb, h1t, e1, cn, e2, r2, ln_g, ln_b, *, tt, ib):
    d, n = h1t.shape
    nexp = wd_bf.shape[0]
    eb = ib * PEER_KEYS
    nk = nexp // eb
    wut_bf = wu_bf.reshape(nk, eb, d).transpose(0, 2, 1)
    last_tile = n // tt - 1
    tokcol = pl.BlockSpec((d, tt), lambda j, s: (0, j))
    tokcol_next = pl.BlockSpec((d, tt), lambda j, s: (0, jnp.minimum(j + (s + 1) // nk, last_tile)))
    tab_i = pl.BlockSpec((PEER_HEADS, ib, tt), lambda j, s: (0, s, j))
    tab_full = pl.BlockSpec((PEER_HEADS, PEER_KEYS, tt), lambda j, s: (0, 0, j))
    full = lambda a: pl.BlockSpec(a.shape, lambda j, s: (0,) * a.ndim)
    return pl.pallas_call(
        functools.partial(_experts_kernel, ib=ib, nk=nk),
        grid=(n // tt, nk),
        in_specs=[pl.BlockSpec((eb, d), lambda j, s: (0, 0)),
                  pl.BlockSpec((eb, d), lambda j, s: ((s + 1) % nk, 0)),
                  pl.BlockSpec((None, d, eb), lambda j, s: (s, 0, 0)),
                  tokcol_next, tokcol, tab_i, tab_i, tab_full, tab_full, full(ln_g), full(ln_b)],
        out_specs=pl.BlockSpec((tt, d), lambda j, s: (j, 0)),
        out_shape=jax.ShapeDtypeStruct((n, d), F32),
        scratch_shapes=[pltpu.VMEM((d, tt), F32),
                        pltpu.VMEM((eb, tt), F32), pltpu.VMEM((eb, tt), F32),
                        pltpu.VMEM((eb, tt), BF16)],
        compiler_params=_params("arbitrary", "arbitrary"),
        name="experts",
    )(wd_bf, wd_bf, wut_bf, h1tb, h1t, e1, cn, e2, r2, ln_g, ln_b)


class _Tiles(NamedTuple):
    inproj_rows: int
    hgrn_steps: int
    s5_steps: int
    route_tokens: int
    expert_tokens: int
    expert_first_keys: int


def _tiles(n, seq):
    pick = lambda size, preferred: preferred if size % preferred == 0 else size
    return _Tiles(inproj_rows=pick(n, 512), hgrn_steps=pick(seq, 512), s5_steps=pick(seq, 64),
                  route_tokens=pick(n, 256), expert_tokens=pick(n, 512), expert_first_keys=8)


def kernel(x, ln0_g, ln0_b, w_in, hg_lb_logits, hg_norm_g, s5_a_re, s5_a_im, s5_log_step,
           s5_b_re, s5_b_im, s5_c_re, s5_c_im, s5_d, w_glu, b_glu, w_out, ln1_g, ln1_b,
           w_query, peer_keys_1, peer_keys_2, peer_down, peer_up, ln2_g, ln2_b):
    assert w_in.shape[0] == 1, "single-layer trunk"
    batch, seq, d = x.shape
    n = batch * seq
    r2d = lambda a: a.reshape(1, -1).astype(F32)
    tiles = _tiles(n, seq)

    x2 = x.reshape(n, d).astype(F32)
    h0, q, gl, k, v, gate, u = _inproj(
        x2, r2d(ln0_g), r2d(ln0_b), _bf(w_in[0]), hg_lb_logits.astype(F32),
        tm=tiles.inproj_rows)

    mix_a = _hgrn(q, gl, k, v, gate, r2d(hg_norm_g[0]), batch=batch,
                  tc=tiles.hgrn_steps)

    abar_re, abar_im, wb, wc = _s5_discretise(
        s5_a_re[0].astype(F32), s5_a_im[0].astype(F32), s5_log_step[0],
        s5_b_re[0].astype(F32), s5_b_im[0].astype(F32), s5_c_re[0], s5_c_im[0])
    mix_b = _s5(u.reshape(batch, seq, -1), abar_re, abar_im, wb, wc, r2d(s5_d[0]),
                _bf(w_glu[0]), r2d(b_glu[0]), tt=tiles.s5_steps).reshape(n, -1)

    h1t, h1tb, e1, cn, e2, r2 = _route(
        mix_a, mix_b, h0, _bf(w_out[0]), r2d(ln1_g[0]), r2d(ln1_b[0]),
        _bf(w_query[0].T), _bf(peer_keys_1[0]), _bf(peer_keys_2[0]), tm=tiles.route_tokens)

    out = _experts(_bf(peer_down[0]), _bf(peer_up[0]), h1tb, h1t, e1, cn, e2, r2,
                   r2d(ln2_g[0]), r2d(ln2_b[0]), tt=tiles.expert_tokens,
                   ib=tiles.expert_first_keys)
    return out.reshape(batch, seq, d).astype(x.dtype)
```

```python
import functools
import math
from typing import NamedTuple

import jax
import jax.numpy as jnp
import numpy as np
from jax import lax
from jax.experimental import pallas as pl
from jax.experimental.pallas import tpu as pltpu

F32 = jnp.float32
BF16 = jnp.bfloat16

CHUNK = 64
SUB = 16
HG_DK = 128
HG_HEADS = 4
PEER_KEYS = 128
PEER_HEADS = 8
PEER_TOPK = 16
PEER_HALF = 128
LN_EPS = 1e-5
RMS_EPS = 1e-6
DN_ALPHA = 2.0 ** 0.25
NEG_INF = float("-inf")
LANES = 128
SUBLANES = 8

VMEM_LIMIT_BYTES = 56 * 1024 * 1024


def _params(*semantics):
    return pltpu.CompilerParams(dimension_semantics=semantics,
                                vmem_limit_bytes=VMEM_LIMIT_BYTES)


def _bf(x):
    return x.astype(BF16)


def _layer_norm(x, g, b):
    mu = jnp.mean(x, axis=-1, keepdims=True)
    xc = x - mu
    var = jnp.mean(xc * xc, axis=-1, keepdims=True)
    return xc * lax.rsqrt(var + LN_EPS) * g + b


def _sigmoid(x):
    return 1.0 / (1.0 + jnp.exp(-x))


def _gelu_tanh(x):
    c = math.sqrt(2.0 / math.pi)
    return x * (0.5 * (1.0 + jnp.tanh(c * (x + 0.044715 * (x * x * x)))))


def _gelu(x):
    k0 = -2.0 * math.sqrt(2.0 / math.pi) * math.log2(math.e)
    w = x * (x * x * (k0 * 0.044715) + k0)
    return x / (1.0 + jnp.exp2(w))


def _inproj_kernel(x_ref, g_ref, b_ref, w_ref, lbl_ref,
                   h_ref, q_ref, gl_ref, k_ref, v_ref, gate_ref, u_ref, *, hw):
    h = _layer_norm(x_ref[...], g_ref[...], b_ref[...])
    h_ref[...] = h
    proj = jnp.dot(_bf(h), w_ref[...], preferred_element_type=F32)
    logits = lbl_ref[...]
    e = jnp.exp(logits - jnp.max(logits, axis=0, keepdims=True))
    lb = e[0:1, :] / jnp.sum(e, axis=0, keepdims=True)
    forget = lb + (1.0 - lb) * _sigmoid(proj[:, hw:2 * hw])
    q_ref[...] = proj[:, 0:hw] * (HG_DK ** -0.5)
    gl_ref[...] = jnp.log(forget)
    k_ref[...] = 1.0 - forget
    v_ref[...] = _bf(proj[:, 2 * hw:3 * hw])
    gate_ref[...] = proj[:, 3 * hw:4 * hw]
    u_ref[...] = proj[:, 4 * hw:]


def _inproj(x2, ln_g, ln_b, w_in_bf, lb_logits, *, tm):
    n, d = x2.shape
    cols = w_in_bf.shape[1]
    hw = lb_logits.shape[1]
    uw = cols - 4 * hw
    row = lambda w: pl.BlockSpec((tm, w), lambda i: (i, 0))
    full = lambda a: pl.BlockSpec(a.shape, lambda i: (0,) * a.ndim)
    wide = lambda w, dt: jax.ShapeDtypeStruct((n, w), dt)
    outs = [wide(d, F32), wide(hw, F32), wide(hw, F32), wide(hw, F32), wide(hw, BF16),
            wide(hw, F32), wide(uw, F32)]
    return pl.pallas_call(
        functools.partial(_inproj_kernel, hw=hw),
        grid=(n // tm,),
        in_specs=[row(d), full(ln_g), full(ln_b), full(w_in_bf), full(lb_logits)],
        out_specs=[row(d)] + [row(hw)] * 5 + [row(uw)],
        out_shape=outs,
        compiler_params=_params("parallel"),
        name="inproj",
    )(x2, ln_g, ln_b, w_in_bf, lb_logits)


def _hgrn_masks():
    t = np.arange(CHUNK)
    le = t[None, :] <= t[:, None]
    same_sub = (t[None, :] // SUB) == (t[:, None] // SUB)
    m = np.concatenate([le, le & same_sub, same_sub], axis=0)
    return m.astype(np.float32)


def _diag_selector():
    s = np.repeat(np.arange(SUB), HG_DK)
    c = np.arange(CHUNK)
    return (s[:, None] == (c[None, :] % SUB)).astype(np.float32)


def _hgrn_kernel(q_ref, g_ref, k_ref, v_ref, gate_ref, ng_ref, m3_ref, sel_ref,
                 o_ref, state_ref, *, tc):
    nch = tc // CHUNK
    nsub = tc // SUB

    @pl.when(pl.program_id(1) == 0)
    def _():
        state_ref[...] = jnp.zeros_like(state_ref)

    g = g_ref[...]
    g_hi = _bf(g)
    r1 = g - g_hi.astype(F32)
    g_mid = _bf(r1)
    g_lo = _bf(r1 - g_mid.astype(F32))
    g_parts = jnp.concatenate([g_hi, g_mid, g_lo], axis=1)
    hw = g.shape[1]
    m3 = m3_ref[...]
    chunk_sums = []
    for c in range(nch):
        r = jnp.dot(m3, g_parts[c * CHUNK:(c + 1) * CHUNK], preferred_element_type=F32)
        chunk_sums.append(r[:, 0:hw] + r[:, hw:2 * hw] + r[:, 2 * hw:])
    part = lambda i: jnp.concatenate(
        [cs_[i * CHUNK:(i + 1) * CHUNK] for cs_ in chunk_sums], axis=0)
    cum_all, cs_all, tot_all = part(0), part(1), part(2)
    row = lax.broadcasted_iota(jnp.int32, (CHUNK, CHUNK), 0)
    col = lax.broadcasted_iota(jnp.int32, (CHUNK, CHUNK), 1)
    blk = row // SUB - col // SUB
    diag_ok = (blk == 0) & (col <= row)
    sel = sel_ref[...]
    ng = ng_ref[...]

    def prepare(h):
        lanes = slice(h * HG_DK, (h + 1) * HG_DK)
        q = q_ref[:, lanes]
        k = k_ref[:, lanes]
        cum = cum_all[:, lanes]
        cs = cs_all[:, lanes]
        tot = tot_all[:, lanes]
        cum3 = cum.reshape(nch, CHUNK, HG_DK)
        last = cum3[:, CHUNK - 1:CHUNK, :]
        q_inter = _bf(q * jnp.exp(cum))
        k_state = _bf((k.reshape(nch, CHUNK, HG_DK) * jnp.exp(last - cum3)).reshape(tc, HG_DK))
        q_sub = q * jnp.exp(cs)
        k_sub = _bf(k * jnp.exp(tot - cs))
        zeros1 = jnp.zeros((SUB, HG_DK), F32)
        tot_p1 = jnp.concatenate([zeros1, tot[:tc - SUB]], axis=0)
        tot_p2 = jnp.concatenate([zeros1, zeros1, tot[:tc - 2 * SUB]], axis=0)
        q_sub2 = q_sub * jnp.exp(tot_p1)
        q_sub3 = q_sub2 * jnp.exp(tot_p2)
        q_subs = (_bf(q_sub), _bf(q_sub2), _bf(q_sub3))
        cs3 = cs.reshape(nsub, SUB, HG_DK)
        q3 = _bf(q).reshape(nsub, SUB, HG_DK)
        k3 = _bf(k).reshape(nsub, SUB, HG_DK)
        pieces = []
        for s in range(SUB):
            decay = jnp.exp(jnp.minimum(cs3 - cs3[:, s:s + 1, :], 0.0))
            pieces.append((q3 * _bf(decay) * k3[:, s:s + 1, :]).reshape(tc, HG_DK))
        diag = jnp.dot(jnp.concatenate(pieces, axis=1), sel,
                       preferred_element_type=F32)
        return q_inter, k_state, q_subs, k_sub, diag, jnp.exp(last)

    def recur(h, prepared):
        q_inter, k_state, q_subs, k_sub, diag, decay_last = prepared
        lanes = slice(h * HG_DK, (h + 1) * HG_DK)
        v_bf = v_ref[:, lanes]
        st = state_ref[h]
        outs = []
        for c in range(nch):
            rows = slice(c * CHUNK, (c + 1) * CHUNK)
            inter = lax.dot_general(q_inter[rows], _bf(st), (((1,), (1,)), ((), ())),
                                    preferred_element_type=F32)
            lhs = jnp.concatenate([qs[rows] for qs in q_subs], axis=0)
            off = lax.dot_general(lhs, k_sub[rows], (((1,), (1,)), ((), ())),
                                  preferred_element_type=F32)
            sc = jnp.where(diag_ok, diag[rows],
                           jnp.where(blk == 1, off[0:CHUNK],
                                     jnp.where(blk == 2, off[CHUNK:2 * CHUNK],
                                               jnp.where(blk == 3, off[2 * CHUNK:], 0.0))))
            intra = jnp.dot(_bf(sc), v_bf[rows], preferred_element_type=F32)
            outs.append(inter + intra)
            upd = lax.dot_general(v_bf[rows], k_state[rows], (((0,), (0,)), ((), ())),
                                  preferred_element_type=F32)
            st = st * decay_last[c] + upd
        state_ref[h] = st
        o = jnp.concatenate(outs, axis=0)
        rms = lax.rsqrt(jnp.mean(o * o, axis=-1, keepdims=True) + RMS_EPS)
        gate = gate_ref[:, lanes]
        o_ref[:, lanes] = _bf(o * rms * ng * (gate * _sigmoid(gate)))

    prepared = prepare(0)
    for h in range(HG_HEADS):
        nxt = prepare(h + 1) if h + 1 < HG_HEADS else None
        recur(h, prepared)
        prepared = nxt


def _hgrn(q, gl, k, v, gate, norm_g, *, batch, tc):
    n, hw = q.shape
    nblk = n // batch // tc
    m3 = jnp.asarray(_hgrn_masks(), dtype=BF16)
    sel = jnp.asarray(_diag_selector(), dtype=BF16)
    row = pl.BlockSpec((tc, hw), lambda b, s: (b * nblk + s, 0))
    full = lambda a: pl.BlockSpec(a.shape, lambda b, s: (0,) * a.ndim)
    return pl.pallas_call(
        functools.partial(_hgrn_kernel, tc=tc),
        grid=(batch, nblk),
        in_specs=[row] * 5 + [full(norm_g), full(m3), full(sel)],
        out_specs=row,
        out_shape=jax.ShapeDtypeStruct((n, hw), BF16),
        scratch_shapes=[pltpu.VMEM((HG_HEADS, HG_DK, HG_DK), F32)],
        compiler_params=_params("parallel", "arbitrary"),
        name="hgrn2",
    )(q, gl, k, v, gate, norm_g, m3, sel)


def _time_major_perm(batch, tt):
    r = np.arange(batch * tt)
    p = np.zeros((batch * tt, batch * tt), np.float32)
    p[r, (r % batch) * tt + r // batch] = 1.0
    return p


def _s5_kernel(u_ref, perm_ref, permt_ref, ar_ref, ai_ref, wb_ref, wc_ref, d_ref, wg_ref, bg_ref,
               o_ref, bu_ref, st_ref, *, tt):
    batch, _, width = u_ref.shape
    ns = ar_ref.shape[1]
    half = ns // 2

    @pl.when(pl.program_id(0) == 0)
    def _():
        st_ref[...] = jnp.zeros_like(st_ref)

    u2 = u_ref[...].reshape(batch * tt, width)
    u_hi = _bf(u2)
    u_lo = _bf(u2 - u_hi.astype(F32))
    perm = perm_ref[...]
    u_hi_tm = jnp.dot(perm, u_hi, preferred_element_type=F32)
    u_tm = u_hi_tm + jnp.dot(perm, u_lo, preferred_element_type=F32)
    cb = 2 * LANES
    nblk = width // cb
    sb = ns // nblk
    for kb in range(nblk):
        ub = _bf(u_hi_tm[:, kb * cb:(kb + 1) * cb])
        for part in range(2):
            cols = slice(part * ns + kb * sb, part * ns + (kb + 1) * sb)
            bu_ref[:, cols] = jnp.dot(ub, wb_ref[kb * cb:(kb + 1) * cb, cols],
                                      preferred_element_type=F32)

    for part in range(2):
        lo = part * half
        ar = jnp.broadcast_to(ar_ref[:, lo:lo + half], (batch, half))
        ai = jnp.broadcast_to(ai_ref[:, lo:lo + half], (batch, half))

        def step(t, carry, lo=lo, ar=ar, ai=ai):
            xr, xi = carry
            rows = pl.ds(pl.multiple_of(t * batch, batch), batch)
            nr = ar * xr - ai * xi + bu_ref[rows, lo:lo + half]
            ni = ar * xi + ai * xr + bu_ref[rows, ns + lo:ns + lo + half]
            bu_ref[rows, lo:lo + half] = nr
            bu_ref[rows, ns + lo:ns + lo + half] = ni
            return nr, ni

        xr, xi = lax.fori_loop(0, tt, step, (st_ref[:, lo:lo + half],
                                             st_ref[:, ns + lo:ns + lo + half]))
        st_ref[:, lo:lo + half] = xr
        st_ref[:, ns + lo:ns + lo + half] = xi

    ys = []
    for ob in range(nblk):
        ocols = slice(ob * cb, (ob + 1) * cb)
        acc = None
        for part in range(2):
            rows = slice(part * ns + ob * sb, part * ns + (ob + 1) * sb)
            t = jnp.dot(_bf(bu_ref[:, rows]), wc_ref[rows, ocols], preferred_element_type=F32)
            acc = t if acc is None else acc + t
        ys.append(acc)
    y = jnp.concatenate(ys, axis=1) + d_ref[...] * u_tm
    z = _gelu_tanh(y)
    gl = jnp.dot(_bf(z), wg_ref[...], preferred_element_type=F32) + bg_ref[...]
    out_tm = _bf(z * _sigmoid(gl))
    out = jnp.dot(permt_ref[...], out_tm, preferred_element_type=F32)
    o_ref[...] = _bf(out).reshape(batch, tt, width)


def _s5(u3, abar_re, abar_im, wb, wc, d, wg, bg, *, tt):
    batch, seq, width = u3.shape
    ns = abar_re.shape[1]
    perm = _time_major_perm(batch, tt)
    perm_bf = jnp.asarray(perm, dtype=BF16)
    permt_bf = jnp.asarray(perm.T, dtype=BF16)
    blk = pl.BlockSpec((batch, tt, width), lambda s: (0, s, 0))
    full = lambda a: pl.BlockSpec(a.shape, lambda s: (0,) * a.ndim)
    consts = (perm_bf, permt_bf, abar_re, abar_im, wb, wc, d, wg, bg)
    return pl.pallas_call(
        functools.partial(_s5_kernel, tt=tt),
        grid=(seq // tt,),
        in_specs=[blk] + [full(a) for a in consts],
        out_specs=blk,
        out_shape=jax.ShapeDtypeStruct(u3.shape, BF16),
        scratch_shapes=[pltpu.VMEM((batch * tt, 2 * ns), F32),
                        pltpu.VMEM((batch, 2 * ns), F32)],
        compiler_params=_params("arbitrary"),
        name="s5",
    )(u3, *consts)


def _s5_discretise(a_re, a_im, log_step, b_re, b_im, c_re, c_im):
    groups, nstate, p = b_re.shape
    step = jnp.exp(log_step.astype(F32))[:, None]
    mag = jnp.exp(step * a_re)
    abar_re = mag * jnp.cos(step * a_im)
    abar_im = mag * jnp.sin(step * a_im)
    num_re, num_im = abar_re - 1.0, abar_im
    den = jnp.square(a_re) + jnp.square(a_im)
    coef_re = ((num_re * a_re + num_im * a_im) / den)[..., None]
    coef_im = ((num_im * a_re - num_re * a_im) / den)[..., None]
    bbar_re = coef_re * b_re - coef_im * b_im
    bbar_im = coef_re * b_im + coef_im * b_re
    eye = jnp.eye(groups, dtype=F32)
    bd = lambda m: jnp.einsum("gnp,gh->gphn", m, eye).reshape(groups * p, groups * nstate)
    wb = jnp.concatenate([bd(bbar_re), bd(bbar_im)], axis=1)
    cd = lambda m: jnp.einsum("gpn,gh->gnhp", m, eye).reshape(groups * nstate, groups * p)
    wc = jnp.concatenate([cd(c_re), -cd(c_im)], axis=0)
    flat = lambda m: m.reshape(1, groups * nstate)
    return flat(abar_re), flat(abar_im), _bf(wb), _bf(wc)


def _oddeven_merge_sort(n):
    pairs = []

    def merge(lo, hi, r):
        step = r * 2
        if step < hi - lo:
            merge(lo, hi, step)
            merge(lo + r, hi, step)
            pairs.extend((i, i + r) for i in range(lo + r, hi - r, step))
        else:
            pairs.append((lo, lo + r))

    def sort(lo, hi):
        if hi - lo >= 1:
            mid = lo + (hi - lo) // 2
            sort(lo, mid)
            sort(mid + 1, hi)
            merge(lo, hi, 1)

    sort(0, n - 1)
    return pairs


def _bitonic_merge(n):
    pairs = []
    dist = n // 2
    while dist >= 1:
        pairs.extend((i, i + dist) for i in range(n) if not i & dist)
        dist //= 2
    return pairs


def _compare_exchange(xs, pairs):
    for i, j in pairs:
        xs[i], xs[j] = jnp.maximum(xs[i], xs[j]), jnp.minimum(xs[i], xs[j])


def _sorted_top(s):
    n = s.shape[0] // SUBLANES
    assert n == PEER_TOPK and n & (n - 1) == 0
    xs = [s[g * SUBLANES:(g + 1) * SUBLANES, :] for g in range(n)]
    _compare_exchange(xs, _oddeven_merge_sort(n))
    shift = SUBLANES // 2
    while shift >= 1:
        rolled = [pltpu.roll(x, shift, 0) for x in xs]
        xs = [jnp.maximum(xs[r], rolled[n - 1 - r]) for r in range(n)]
        _compare_exchange(xs, _bitonic_merge(n))
        shift //= 2
    return [x[0:1, :] for x in xs]


def _rank_among(vals, s):
    rank = jnp.full(s.shape, float(len(vals)), F32)
    for b in range(len(vals) - 1, -1, -1):
        rank = jnp.where(vals[b] > s, rank, float(b))
    return rank


def _stack_rows(rows, nrows):
    t = rows[0].shape[1]
    idx = lax.broadcasted_iota(jnp.int32, (nrows, t), 0)
    out = jnp.zeros((nrows, t), F32)
    for r, v in enumerate(rows):
        out = jnp.where(idx == r, v, out)
    return out


def _route_head(s1, s2):
    v1 = _sorted_top(s1)
    v2 = _sorted_top(s2)
    rank2 = _rank_among(v2, s2)
    t = s1.shape[1]
    v2_lo = _stack_rows(v2[:8], 8)
    v2_hi = _stack_rows(v2[8:], 8)
    v1_hi = _stack_rows(v1[8:], 8)
    idx8 = lax.broadcasted_iota(jnp.int32, (8, t), 0)
    cand = [v1[0] + v2_lo, v1[0] + v2_hi]
    for a in range(1, 8):
        nb = PEER_TOPK // (a + 1)
        c = v1[a] + v2_lo
        cand.append(c if nb >= 8 else jnp.where(idx8 < nb, c, NEG_INF))
    cand.append(v1_hi + v2[0])
    cur = jnp.concatenate(cand, axis=0)
    top = v1[0] + v2[0]
    z = jnp.zeros_like(top)
    tau = top
    for _ in range(PEER_TOPK):
        tau = jnp.max(cur, axis=0, keepdims=True)
        z = z + jnp.exp(tau - top)
        cur = jnp.where(cur == tau, NEG_INF, cur)
    b0 = 4
    cnt = jnp.zeros(s1.shape, F32)
    for b in range(b0):
        cnt = cnt + jnp.where(s1 + v2[b] >= tau, 1.0, 0.0)
    for a in range(PEER_TOPK // (b0 + 1)):
        extra = jnp.zeros_like(tau)
        for b in range(b0, PEER_TOPK):
            extra = extra + jnp.where(v1[a] + v2[b] >= tau, 1.0, 0.0)
        cnt = cnt + jnp.where(s1 == v1[a], extra, 0.0)
    e1 = jnp.exp(s1 - v1[0])
    e2 = jnp.exp(s2 - v2[0]) / z
    return e1, cnt, e2, rank2


def _route_kernel(ma_ref, mb_ref, h0_ref, wo_ref, g_ref, b_ref, wq_ref, k1_ref, k2_ref,
                  h1t_ref, h1tb_ref, e1_ref, cn_ref, e2_ref, r2_ref):
    hw = ma_ref.shape[1]
    mixed = jnp.dot(ma_ref[...], wo_ref[0:hw, :], preferred_element_type=F32)
    mixed = mixed + jnp.dot(mb_ref[...], wo_ref[hw:, :], preferred_element_type=F32)
    h1 = _layer_norm(DN_ALPHA * h0_ref[...] + mixed, g_ref[...], b_ref[...])
    h1t = h1.T
    h1t_ref[...] = h1t
    h1tb = _bf(h1t)
    h1tb_ref[...] = h1tb
    qt = jnp.dot(wq_ref[...], h1tb, preferred_element_type=F32)
    k1 = k1_ref[...]
    k2 = k2_ref[...]
    for h in range(PEER_HEADS):
        base = h * 2 * PEER_HALF
        s1 = jnp.dot(k1, _bf(qt[base:base + PEER_HALF]), preferred_element_type=F32)
        s2 = jnp.dot(k2, _bf(qt[base + PEER_HALF:base + 2 * PEER_HALF]),
                     preferred_element_type=F32)
        e1, cnt, e2, rank2 = _route_head(s1, s2)
        e1_ref[h] = e1
        cn_ref[h] = cnt
        e2_ref[h] = _bf(e2)
        r2_ref[h] = _bf(rank2)


def _route(mix_a, mix_b, h0, w_out_bf, ln_g, ln_b, wq_t_bf, k1_bf, k2_bf, *, tm):
    n, d = h0.shape
    hw = mix_a.shape[1]
    row = lambda w: pl.BlockSpec((tm, w), lambda i: (i, 0))
    full = lambda a: pl.BlockSpec(a.shape, lambda i: (0,) * a.ndim)
    colblk = pl.BlockSpec((d, tm), lambda i: (0, i))
    tab = pl.BlockSpec((PEER_HEADS, PEER_KEYS, tm), lambda i: (0, 0, i))
    tab_shape = (PEER_HEADS, PEER_KEYS, n)
    return pl.pallas_call(
        _route_kernel,
        grid=(n // tm,),
        in_specs=[row(hw), row(hw), row(d)] +
                 [full(a) for a in (w_out_bf, ln_g, ln_b, wq_t_bf, k1_bf, k2_bf)],
        out_specs=[colblk, colblk, tab, tab, tab, tab],
        out_shape=[jax.ShapeDtypeStruct((d, n), F32), jax.ShapeDtypeStruct((d, n), BF16),
                   jax.ShapeDtypeStruct(tab_shape, F32), jax.ShapeDtypeStruct(tab_shape, F32),
                   jax.ShapeDtypeStruct(tab_shape, BF16), jax.ShapeDtypeStruct(tab_shape, BF16)],
        compiler_params=_params("parallel"),
        name="route",
    )(mix_a, mix_b, h0, w_out_bf, ln_g, ln_b, wq_t_bf, k1_bf, k2_bf)


def _experts_kernel(wd0_ref, wd_ref, wut_ref, h1tbn_ref, h1t_ref, e1_ref, cn_ref, e2_ref, r2_ref,
                    g_ref, b_ref, o_ref, acc_ref, a0_ref, a1_ref, p_ref, *, ib, nk):
    j = pl.program_id(0)
    s = pl.program_id(1)
    tt = h1tbn_ref.shape[1]
    half = tt // 2

    @pl.when(s == 0)
    def _():
        acc_ref[...] = jnp.zeros_like(acc_ref)

    @pl.when((j == 0) & (s == 0))
    def _():
        a0_ref[...] = jnp.dot(wd0_ref[...], h1tbn_ref[...], preferred_element_type=F32)

    def step(a_cur, a_nxt):
        for hf in range(tt // half):
            hl = pl.ds(hf * half, half)
            for ii in range(ib):
                gsum = jnp.zeros((PEER_KEYS, half), BF16)
                for h in range(PEER_HEADS):
                    e1 = _bf(jnp.broadcast_to(e1_ref[h, ii:ii + 1, hl], gsum.shape))
                    cn = _bf(jnp.broadcast_to(cn_ref[h, ii:ii + 1, hl], gsum.shape))
                    val = e2_ref[h, :, hl] * e1
                    gsum = gsum + jnp.where(r2_ref[h, :, hl] < cn, val, jnp.zeros_like(val))
                rows = pl.ds(ii * PEER_KEYS, PEER_KEYS)
                p_ref[rows, hl] = _bf(_gelu(a_cur[rows, hl])) * gsum
            a_nxt[:, hl] = jnp.dot(wd_ref[...], h1tbn_ref[:, hl], preferred_element_type=F32)
            acc_ref[:, hl] += jnp.dot(wut_ref[...], p_ref[:, hl], preferred_element_type=F32)

    @pl.when(s % 2 == 0)
    def _():
        step(a0_ref, a1_ref)

    @pl.when(s % 2 == 1)
    def _():
        step(a1_ref, a0_ref)

    @pl.when(s == nk - 1)
    def _():
        y = DN_ALPHA * h1t_ref[...] + acc_ref[...]
        mu = jnp.mean(y, axis=0, keepdims=True)
        yc = y - mu
        var = jnp.mean(yc * yc, axis=0, keepdims=True)
        yn = yc * lax.rsqrt(var + LN_EPS)
        o_ref[...] = yn.T * g_ref[...] + b_ref[...]


def _experts(wd_bf, wu_bf, h1tb, h1t, e1, cn, e2, r2, ln_g, ln_b, *, tt, ib):
    d, n = h1t.shape
    nexp = wd_bf.shape[0]
    eb = ib * PEER_KEYS
    nk = nexp // eb
    wut_bf = wu_bf.reshape(nk, eb, d).transpose(0, 2, 1)
    last_tile = n // tt - 1
    tokcol = pl.BlockSpec((d, tt), lambda j, s: (0, j))
    tokcol_next = pl.BlockSpec((d, tt), lambda j, s: (0, jnp.minimum(j + (s + 1) // nk, last_tile)))
    tab_i = pl.BlockSpec((PEER_HEADS, ib, tt), lambda j, s: (0, s, j))
    tab_full = pl.BlockSpec((PEER_HEADS, PEER_KEYS, tt), lambda j, s: (0, 0, j))
    full = lambda a: pl.BlockSpec(a.shape, lambda j, s: (0,) * a.ndim)
    return pl.pallas_call(
        functools.partial(_experts_kernel, ib=ib, nk=nk),
        grid=(n // tt, nk),
        in_specs=[pl.BlockSpec((eb, d), lambda j, s: (0, 0)),
                  pl.BlockSpec((eb, d), lambda j, s: ((s + 1) % nk, 0)),
                  pl.BlockSpec((None, d, eb), lambda j, s: (s, 0, 0)),
                  tokcol_next, tokcol, tab_i, tab_i, tab_full, tab_full, full(ln_g), full(ln_b)],
        out_specs=pl.BlockSpec((tt, d), lambda j, s: (j, 0)),
        out_shape=jax.ShapeDtypeStruct((n, d), F32),
        scratch_shapes=[pltpu.VMEM((d, tt), F32),
                        pltpu.VMEM((eb, tt), F32), pltpu.VMEM((eb, tt), F32),
                        pltpu.VMEM((eb, tt), BF16)],
        compiler_params=_params("arbitrary", "arbitrary"),
        name="experts",
    )(wd_bf, wd_bf, wut_bf, h1tb, h1t, e1, cn, e2, r2, ln_g, ln_b)


class _Tiles(NamedTuple):
    inproj_rows: int
    hgrn_steps: int
    s5_steps: int
    route_tokens: int
    expert_tokens: int
    expert_first_keys: int


def _tiles(n, seq):
    pick = lambda size, preferred: preferred if size % preferred == 0 else size
    return _Tiles(inproj_rows=pick(n, 512), hgrn_steps=pick(seq, 512), s5_steps=pick(seq, 64),
                  route_tokens=pick(n, 256), expert_tokens=pick(n, 512), expert_first_keys=16)


def kernel(x, ln0_g, ln0_b, w_in, hg_lb_logits, hg_norm_g, s5_a_re, s5_a_im, s5_log_step,
           s5_b_re, s5_b_im, s5_c_re, s5_c_im, s5_d, w_glu, b_glu, w_out, ln1_g, ln1_b,
           w_query, peer_keys_1, peer_keys_2, peer_down, peer_up, ln2_g, ln2_b):
    assert w_in.shape[0] == 1, "single-layer trunk"
    batch, seq, d = x.shape
    n = batch * seq
    r2d = lambda a: a.reshape(1, -1).astype(F32)
    tiles = _tiles(n, seq)

    x2 = x.reshape(n, d).astype(F32)
    h0, q, gl, k, v, gate, u = _inproj(
        x2, r2d(ln0_g), r2d(ln0_b), _bf(w_in[0]), hg_lb_logits.astype(F32),
        tm=tiles.inproj_rows)

    mix_a = _hgrn(q, gl, k, v, gate, r2d(hg_norm_g[0]), batch=batch,
                  tc=tiles.hgrn_steps)

    abar_re, abar_im, wb, wc = _s5_discretise(
        s5_a_re[0].astype(F32), s5_a_im[0].astype(F32), s5_log_step[0],
        s5_b_re[0].astype(F32), s5_b_im[0].astype(F32), s5_c_re[0], s5_c_im[0])
    mix_b = _s5(u.reshape(batch, seq, -1), abar_re, abar_im, wb, wc, r2d(s5_d[0]),
                _bf(w_glu[0]), r2d(b_glu[0]), tt=tiles.s5_steps).reshape(n, -1)

    h1t, h1tb, e1, cn, e2, r2 = _route(
        mix_a, mix_b, h0, _bf(w_out[0]), r2d(ln1_g[0]), r2d(ln1_b[0]),
        _bf(w_query[0].T), _bf(peer_keys_1[0]), _bf(peer_keys_2[0]), tm=tiles.route_tokens)

    out = _experts(_bf(peer_down[0]), _bf(peer_up[0]), h1tb, h1t, e1, cn, e2, r2,
                   r2d(ln2_g[0]), r2d(ln2_b[0]), tt=tiles.expert_tokens,
                   ib=tiles.expert_first_keys)
    return out.reshape(batch, seq, d).astype(x.dtype)
```

```python
import functools
import math
from typing import NamedTuple

import jax
import jax.numpy as jnp
import numpy as np
from jax import lax
from jax.experimental import pallas as pl
from jax.experimental.pallas import tpu as pltpu

F32 = jnp.float32
BF16 = jnp.bfloat16

CHUNK = 64
SUB = 16
HG_DK = 128
HG_HEADS = 4
PEER_KEYS = 128
PEER_HEADS = 8
PEER_TOPK = 16
PEER_HALF = 128
LN_EPS = 1e-5
RMS_EPS = 1e-6
DN_ALPHA = 2.0 ** 0.25
NEG_INF = float("-inf")
LANES = 128
SUBLANES = 8

VMEM_LIMIT_BYTES = 56 * 1024 * 1024


def _params(*semantics):
    return pltpu.CompilerParams(dimension_semantics=semantics,
                                vmem_limit_bytes=VMEM_LIMIT_BYTES)


def _bf(x):
    return x.astype(BF16)


def _layer_norm(x, g, b):
    mu = jnp.mean(x, axis=-1, keepdims=True)
    xc = x - mu
    var = jnp.mean(xc * xc, axis=-1, keepdims=True)
    return xc * lax.rsqrt(var + LN_EPS) * g + b


def _sigmoid(x):
    return 1.0 / (1.0 + jnp.exp(-x))


def _gelu_tanh(x):
    c = math.sqrt(2.0 / math.pi)
    return x * (0.5 * (1.0 + jnp.tanh(c * (x + 0.044715 * (x * x * x)))))


def _gelu(x):
    k0 = -2.0 * math.sqrt(2.0 / math.pi) * math.log2(math.e)
    w = x * (x * x * (k0 * 0.044715) + k0)
    return x / (1.0 + jnp.exp2(w))


def _inproj_kernel(x_ref, g_ref, b_ref, w_ref, lbl_ref,
                   h_ref, q_ref, gl_ref, k_ref, v_ref, gate_ref, u_ref, *, hw):
    h = _layer_norm(x_ref[...], g_ref[...], b_ref[...])
    h_ref[...] = h
    proj = jnp.dot(_bf(h), w_ref[...], preferred_element_type=F32)
    logits = lbl_ref[...]
    e = jnp.exp(logits - jnp.max(logits, axis=0, keepdims=True))
    lb = e[0:1, :] / jnp.sum(e, axis=0, keepdims=True)
    forget = lb + (1.0 - lb) * _sigmoid(proj[:, hw:2 * hw])
    q_ref[...] = proj[:, 0:hw] * (HG_DK ** -0.5)
    gl_ref[...] = jnp.log(forget)
    k_ref[...] = 1.0 - forget
    v_ref[...] = _bf(proj[:, 2 * hw:3 * hw])
    gate_ref[...] = proj[:, 3 * hw:4 * hw]
    u_ref[...] = proj[:, 4 * hw:]


def _inproj(x2, ln_g, ln_b, w_in_bf, lb_logits, *, tm):
    n, d = x2.shape
    cols = w_in_bf.shape[1]
    hw = lb_logits.shape[1]
    uw = cols - 4 * hw
    row = lambda w: pl.BlockSpec((tm, w), lambda i: (i, 0))
    full = lambda a: pl.BlockSpec(a.shape, lambda i: (0,) * a.ndim)
    wide = lambda w, dt: jax.ShapeDtypeStruct((n, w), dt)
    outs = [wide(d, F32), wide(hw, F32), wide(hw, F32), wide(hw, F32), wide(hw, BF16),
            wide(hw, F32), wide(uw, F32)]
    return pl.pallas_call(
        functools.partial(_inproj_kernel, hw=hw),
        grid=(n // tm,),
        in_specs=[row(d), full(ln_g), full(ln_b), full(w_in_bf), full(lb_logits)],
        out_specs=[row(d)] + [row(hw)] * 5 + [row(uw)],
        out_shape=outs,
        compiler_params=_params("parallel"),
        name="inproj",
    )(x2, ln_g, ln_b, w_in_bf, lb_logits)


def _hgrn_masks():
    t = np.arange(CHUNK)
    le = t[None, :] <= t[:, None]
    same_sub = (t[None, :] // SUB) == (t[:, None] // SUB)
    m = np.concatenate([le, le & same_sub, same_sub], axis=0)
    return m.astype(np.float32)


def _diag_selector():
    s = np.repeat(np.arange(SUB), HG_DK)
    c = np.arange(CHUNK)
    return (s[:, None] == (c[None, :] % SUB)).astype(np.float32)


def _hgrn_kernel(q_ref, g_ref, k_ref, v_ref, gate_ref, ng_ref, m3_ref, sel_ref,
                 o_ref, state_ref, *, tc):
    nch = tc // CHUNK
    nsub = tc // SUB

    @pl.when(pl.program_id(1) == 0)
    def _():
        state_ref[...] = jnp.zeros_like(state_ref)

    g = g_ref[...]
    g_hi = _bf(g)
    r1 = g - g_hi.astype(F32)
    g_mid = _bf(r1)
    g_lo = _bf(r1 - g_mid.astype(F32))
    g_parts = jnp.concatenate([g_hi, g_mid, g_lo], axis=1)
    hw = g.shape[1]
    m3 = m3_ref[...]
    chunk_sums = []
    for c in range(nch):
        r = jnp.dot(m3, g_parts[c * CHUNK:(c + 1) * CHUNK], preferred_element_type=F32)
        chunk_sums.append(r[:, 0:hw] + r[:, hw:2 * hw] + r[:, 2 * hw:])
    part = lambda i: jnp.concatenate(
        [cs_[i * CHUNK:(i + 1) * CHUNK] for cs_ in chunk_sums], axis=0)
    cum_all, cs_all, tot_all = part(0), part(1), part(2)
    row = lax.broadcasted_iota(jnp.int32, (CHUNK, CHUNK), 0)
    col = lax.broadcasted_iota(jnp.int32, (CHUNK, CHUNK), 1)
    blk = row // SUB - col // SUB
    diag_ok = (blk == 0) & (col <= row)
    sel = sel_ref[...]
    ng = ng_ref[...]

    def prepare(h):
        lanes = slice(h * HG_DK, (h + 1) * HG_DK)
        q = q_ref[:, lanes]
        k = k_ref[:, lanes]
        cum = cum_all[:, lanes]
        cs = cs_all[:, lanes]
        tot = tot_all[:, lanes]
        cum3 = cum.reshape(nch, CHUNK, HG_DK)
        last = cum3[:, CHUNK - 1:CHUNK, :]
        q_inter = _bf(q * jnp.exp(cum))
        k_state = _bf((k.reshape(nch, CHUNK, HG_DK) * jnp.exp(last - cum3)).reshape(tc, HG_DK))
        q_sub = q * jnp.exp(cs)
        k_sub = _bf(k * jnp.exp(tot - cs))
        zeros1 = jnp.zeros((SUB, HG_DK), F32)
        tot_p1 = jnp.concatenate([zeros1, tot[:tc - SUB]], axis=0)
        tot_p2 = jnp.concatenate([zeros1, zeros1, tot[:tc - 2 * SUB]], axis=0)
        q_sub2 = q_sub * jnp.exp(tot_p1)
        q_sub3 = q_sub2 * jnp.exp(tot_p2)
        q_subs = (_bf(q_sub), _bf(q_sub2), _bf(q_sub3))
        cs3 = cs.reshape(nsub, SUB, HG_DK)
        q3 = _bf(q).reshape(nsub, SUB, HG_DK)
        k3 = _bf(k).reshape(nsub, SUB, HG_DK)
        pieces = []
        for s in range(SUB):
            decay = jnp.exp(jnp.minimum(cs3 - cs3[:, s:s + 1, :], 0.0))
            pieces.append((q3 * _bf(decay) * k3[:, s:s + 1, :]).reshape(tc, HG_DK))
        diag = jnp.dot(jnp.concatenate(pieces, axis=1), sel,
                       preferred_element_type=F32)
        return q_inter, k_state, q_subs, k_sub, diag, jnp.exp(last)

    def recur(h, prepared):
        q_inter, k_state, q_subs, k_sub, diag, decay_last = prepared
        lanes = slice(h * HG_DK, (h + 1) * HG_DK)
        v_bf = v_ref[:, lanes]
        st = state_ref[h]
        outs = []
        for c in range(nch):
            rows = slice(c * CHUNK, (c + 1) * CHUNK)
            inter = lax.dot_general(q_inter[rows], _bf(st), (((1,), (1,)), ((), ())),
                                    preferred_element_type=F32)
            lhs = jnp.concatenate([qs[rows] for qs in q_subs], axis=0)
            off = lax.dot_general(lhs, k_sub[rows], (((1,), (1,)), ((), ())),
                                  preferred_element_type=F32)
            sc = jnp.where(diag_ok, diag[rows],
                           jnp.where(blk == 1, off[0:CHUNK],
                                     jnp.where(blk == 2, off[CHUNK:2 * CHUNK],
                                               jnp.where(blk == 3, off[2 * CHUNK:], 0.0))))
            intra = jnp.dot(_bf(sc), v_bf[rows], preferred_element_type=F32)
            outs.append(inter + intra)
            upd = lax.dot_general(v_bf[rows], k_state[rows], (((0,), (0,)), ((), ())),
                                  preferred_element_type=F32)
            st = st * decay_last[c] + upd
        state_ref[h] = st
        o = jnp.concatenate(outs, axis=0)
        rms = lax.rsqrt(jnp.mean(o * o, axis=-1, keepdims=True) + RMS_EPS)
        gate = gate_ref[:, lanes]
        o_ref[:, lanes] = _bf(o * rms * ng * (gate * _sigmoid(gate)))

    prepared = prepare(0)
    for h in range(HG_HEADS):
        nxt = prepare(h + 1) if h + 1 < HG_HEADS else None
        recur(h, prepared)
        prepared = nxt


def _hgrn(q, gl, k, v, gate, norm_g, *, batch, tc):
    n, hw = q.shape
    nblk = n // batch // tc
    m3 = jnp.asarray(_hgrn_masks(), dtype=BF16)
    sel = jnp.asarray(_diag_selector(), dtype=BF16)
    row = pl.BlockSpec((tc, hw), lambda b, s: (b * nblk + s, 0))
    full = lambda a: pl.BlockSpec(a.shape, lambda b, s: (0,) * a.ndim)
    return pl.pallas_call(
        functools.partial(_hgrn_kernel, tc=tc),
        grid=(batch, nblk),
        in_specs=[row] * 5 + [full(norm_g), full(m3), full(sel)],
        out_specs=row,
        out_shape=jax.ShapeDtypeStruct((n, hw), BF16),
        scratch_shapes=[pltpu.VMEM((HG_HEADS, HG_DK, HG_DK), F32)],
        compiler_params=_params("parallel", "arbitrary"),
        name="hgrn2",
    )(q, gl, k, v, gate, norm_g, m3, sel)


def _time_major_perm(batch, tt):
    r = np.arange(batch * tt)
    p = np.zeros((batch * tt, batch * tt), np.float32)
    p[r, (r % batch) * tt + r // batch] = 1.0
    return p


def _s5_kernel(u_ref, perm_ref, permt_ref, ar_ref, ai_ref, wb_ref, wc_ref, d_ref, wg_ref, bg_ref,
               o_ref, bu_ref, st_ref, *, tt):
    batch, _, width = u_ref.shape
    ns = ar_ref.shape[1]
    half = ns // 2

    @pl.when(pl.program_id(0) == 0)
    def _():
        st_ref[...] = jnp.zeros_like(st_ref)

    u2 = u_ref[...].reshape(batch * tt, width)
    u_hi = _bf(u2)
    u_lo = _bf(u2 - u_hi.astype(F32))
    perm = perm_ref[...]
    u_hi_tm = jnp.dot(perm, u_hi, preferred_element_type=F32)
    u_tm = u_hi_tm + jnp.dot(perm, u_lo, preferred_element_type=F32)
    cb = 2 * LANES
    nblk = width // cb
    sb = ns // nblk
    for kb in range(nblk):
        ub = _bf(u_hi_tm[:, kb * cb:(kb + 1) * cb])
        for part in range(2):
            cols = slice(part * ns + kb * sb, part * ns + (kb + 1) * sb)
            bu_ref[:, cols] = jnp.dot(ub, wb_ref[kb * cb:(kb + 1) * cb, cols],
                                      preferred_element_type=F32)

    for part in range(2):
        lo = part * half
        ar = jnp.broadcast_to(ar_ref[:, lo:lo + half], (batch, half))
        ai = jnp.broadcast_to(ai_ref[:, lo:lo + half], (batch, half))

        def step(t, carry, lo=lo, ar=ar, ai=ai):
            xr, xi = carry
            rows = pl.ds(pl.multiple_of(t * batch, batch), batch)
            nr = ar * xr - ai * xi + bu_ref[rows, lo:lo + half]
            ni = ar * xi + ai * xr + bu_ref[rows, ns + lo:ns + lo + half]
            bu_ref[rows, lo:lo + half] = nr
            bu_ref[rows, ns + lo:ns + lo + half] = ni
            return nr, ni

        xr, xi = lax.fori_loop(0, tt, step, (st_ref[:, lo:lo + half],
                                             st_ref[:, ns + lo:ns + lo + half]))
        st_ref[:, lo:lo + half] = xr
        st_ref[:, ns + lo:ns + lo + half] = xi

    ys = []
    for ob in range(nblk):
        ocols = slice(ob * cb, (ob + 1) * cb)
        acc = None
        for part in range(2):
            rows = slice(part * ns + ob * sb, part * ns + (ob + 1) * sb)
            t = jnp.dot(_bf(bu_ref[:, rows]), wc_ref[rows, ocols], preferred_element_type=F32)
            acc = t if acc is None else acc + t
        ys.append(acc)
    y = jnp.concatenate(ys, axis=1) + d_ref[...] * u_tm
    z = _gelu_tanh(y)
    gl = jnp.dot(_bf(z), wg_ref[...], preferred_element_type=F32) + bg_ref[...]
    out_tm = _bf(z * _sigmoid(gl))
    out = jnp.dot(permt_ref[...], out_tm, preferred_element_type=F32)
    o_ref[...] = _bf(out).reshape(batch, tt, width)


def _s5(u3, abar_re, abar_im, wb, wc, d, wg, bg, *, tt):
    batch, seq, width = u3.shape
    ns = abar_re.shape[1]
    perm = _time_major_perm(batch, tt)
    perm_bf = jnp.asarray(perm, dtype=BF16)
    permt_bf = jnp.asarray(perm.T, dtype=BF16)
    blk = pl.BlockSpec((batch, tt, width), lambda s: (0, s, 0))
    full = lambda a: pl.BlockSpec(a.shape, lambda s: (0,) * a.ndim)
    consts = (perm_bf, permt_bf, abar_re, abar_im, wb, wc, d, wg, bg)
    return pl.pallas_call(
        functools.partial(_s5_kernel, tt=tt),
        grid=(seq // tt,),
        in_specs=[blk] + [full(a) for a in consts],
        out_specs=blk,
        out_shape=jax.ShapeDtypeStruct(u3.shape, BF16),
        scratch_shapes=[pltpu.VMEM((batch * tt, 2 * ns), F32),
                        pltpu.VMEM((batch, 2 * ns), F32)],
        compiler_params=_params("arbitrary"),
        name="s5",
    )(u3, *consts)


def _s5_discretise(a_re, a_im, log_step, b_re, b_im, c_re, c_im):
    groups, nstate, p = b_re.shape
    step = jnp.exp(log_step.astype(F32))[:, None]
    mag = jnp.exp(step * a_re)
    abar_re = mag * jnp.cos(step * a_im)
    abar_im = mag * jnp.sin(step * a_im)
    num_re, num_im = abar_re - 1.0, abar_im
    den = jnp.square(a_re) + jnp.square(a_im)
    coef_re = ((num_re * a_re + num_im * a_im) / den)[..., None]
    coef_im = ((num_im * a_re - num_re * a_im) / den)[..., None]
    bbar_re = coef_re * b_re - coef_im * b_im
    bbar_im = coef_re * b_im + coef_im * b_re
    eye = jnp.eye(groups, dtype=F32)
    bd = lambda m: jnp.einsum("gnp,gh->gphn", m, eye).reshape(groups * p, groups * nstate)
    wb = jnp.concatenate([bd(bbar_re), bd(bbar_im)], axis=1)
    cd = lambda m: jnp.einsum("gpn,gh->gnhp", m, eye).reshape(groups * nstate, groups * p)
    wc = jnp.concatenate([cd(c_re), -cd(c_im)], axis=0)
    flat = lambda m: m.reshape(1, groups * nstate)
    return flat(abar_re), flat(abar_im), _bf(wb), _bf(wc)


def _oddeven_merge_sort(n):
    pairs = []

    def merge(lo, hi, r):
        step = r * 2
        if step < hi - lo:
            merge(lo, hi, step)
            merge(lo + r, hi, step)
            pairs.extend((i, i + r) for i in range(lo + r, hi - r, step))
        else:
            pairs.append((lo, lo + r))

    def sort(lo, hi):
        if hi - lo >= 1:
            mid = lo + (hi - lo) // 2
            sort(lo, mid)
            sort(mid + 1, hi)
            merge(lo, hi, 1)

    sort(0, n - 1)
    return pairs


def _bitonic_merge(n):
    pairs = []
    dist = n // 2
    while dist >= 1:
        pairs.extend((i, i + dist) for i in range(n) if not i & dist)
        dist //= 2
    return pairs


def _compare_exchange(xs, pairs):
    for i, j in pairs:
        xs[i], xs[j] = jnp.maximum(xs[i], xs[j]), jnp.minimum(xs[i], xs[j])


def _sorted_top(s):
    n = s.shape[0] // SUBLANES
    assert n == PEER_TOPK and n & (n - 1) == 0
    xs = [s[g * SUBLANES:(g + 1) * SUBLANES, :] for g in range(n)]
    _compare_exchange(xs, _oddeven_merge_sort(n))
    shift = SUBLANES // 2
    while shift >= 1:
        rolled = [pltpu.roll(x, shift, 0) for x in xs]
        xs = [jnp.maximum(xs[r], rolled[n - 1 - r]) for r in range(n)]
        _compare_exchange(xs, _bitonic_merge(n))
        shift //= 2
    return [x[0:1, :] for x in xs]


def _rank_among(vals, s):
    rank = jnp.full(s.shape, float(len(vals)), F32)
    for b in range(len(vals) - 1, -1, -1):
        rank = jnp.where(vals[b] > s, rank, float(b))
    return rank


def _stack_rows(rows, nrows):
    t = rows[0].shape[1]
    idx = lax.broadcasted_iota(jnp.int32, (nrows, t), 0)
    out = jnp.zeros((nrows, t), F32)
    for r, v in enumerate(rows):
        out = jnp.where(idx == r, v, out)
    return out


def _route_head(s1, s2):
    v1 = _sorted_top(s1)
    v2 = _sorted_top(s2)
    rank2 = _rank_among(v2, s2)
    t = s1.shape[1]
    v2_lo = _stack_rows(v2[:8], 8)
    v2_hi = _stack_rows(v2[8:], 8)
    v1_hi = _stack_rows(v1[8:], 8)
    idx8 = lax.broadcasted_iota(jnp.int32, (8, t), 0)
    cand = [v1[0] + v2_lo, v1[0] + v2_hi]
    for a in range(1, 8):
        nb = PEER_TOPK // (a + 1)
        c = v1[a] + v2_lo
        cand.append(c if nb >= 8 else jnp.where(idx8 < nb, c, NEG_INF))
    cand.append(v1_hi + v2[0])
    cur = jnp.concatenate(cand, axis=0)
    top = v1[0] + v2[0]
    z = jnp.zeros_like(top)
    tau = top
    for _ in range(PEER_TOPK):
        tau = jnp.max(cur, axis=0, keepdims=True)
        z = z + jnp.exp(tau - top)
        cur = jnp.where(cur == tau, NEG_INF, cur)
    b0 = 4
    cnt = jnp.zeros(s1.shape, F32)
    for b in range(b0):
        cnt = cnt + jnp.where(s1 + v2[b] >= tau, 1.0, 0.0)
    for a in range(PEER_TOPK // (b0 + 1)):
        extra = jnp.zeros_like(tau)
        for b in range(b0, PEER_TOPK):
            extra = extra + jnp.where(v1[a] + v2[b] >= tau, 1.0, 0.0)
        cnt = cnt + jnp.where(s1 == v1[a], extra, 0.0)
    e1 = jnp.exp(s1 - v1[0])
    e2 = jnp.exp(s2 - v2[0]) / z
    return e1, cnt, e2, rank2


def _route_kernel(ma_ref, mb_ref, h0_ref, wo_ref, g_ref, b_ref, wq_ref, k1_ref, k2_ref,
                  h1t_ref, h1tb_ref, e1_ref, cn_ref, e2_ref, r2_ref):
    hw = ma_ref.shape[1]
    mixed = jnp.dot(ma_ref[...], wo_ref[0:hw, :], preferred_element_type=F32)
    mixed = mixed + jnp.dot(mb_ref[...], wo_ref[hw:, :], preferred_element_type=F32)
    h1 = _layer_norm(DN_ALPHA * h0_ref[...] + mixed, g_ref[...], b_ref[...])
    h1t = h1.T
    h1t_ref[...] = h1t
    h1tb = _bf(h1t)
    h1tb_ref[...] = h1tb
    qt = jnp.dot(wq_ref[...], h1tb, preferred_element_type=F32)
    k1 = k1_ref[...]
    k2 = k2_ref[...]
    for h in range(PEER_HEADS):
        base = h * 2 * PEER_HALF
        s1 = jnp.dot(k1, _bf(qt[base:base + PEER_HALF]), preferred_element_type=F32)
        s2 = jnp.dot(k2, _bf(qt[base + PEER_HALF:base + 2 * PEER_HALF]),
                     preferred_element_type=F32)
        e1, cnt, e2, rank2 = _route_head(s1, s2)
        e1_ref[h] = e1
        cn_ref[h] = cnt
        e2_ref[h] = _bf(e2)
        r2_ref[h] = _bf(rank2)


def _route(mix_a, mix_b, h0, w_out_bf, ln_g, ln_b, wq_t_bf, k1_bf, k2_bf, *, tm):
    n, d = h0.shape
    hw = mix_a.shape[1]
    row = lambda w: pl.BlockSpec((tm, w), lambda i: (i, 0))
    full = lambda a: pl.BlockSpec(a.shape, lambda i: (0,) * a.ndim)
    colblk = pl.BlockSpec((d, tm), lambda i: (0, i))
    tab = pl.BlockSpec((PEER_HEADS, PEER_KEYS, tm), lambda i: (0, 0, i))
    tab_shape = (PEER_HEADS, PEER_KEYS, n)
    return pl.pallas_call(
        _route_kernel,
        grid=(n // tm,),
        in_specs=[row(hw), row(hw), row(d)] +
                 [full(a) for a in (w_out_bf, ln_g, ln_b, wq_t_bf, k1_bf, k2_bf)],
        out_specs=[colblk, colblk, tab, tab, tab, tab],
        out_shape=[jax.ShapeDtypeStruct((d, n), F32), jax.ShapeDtypeStruct((d, n), BF16),
                   jax.ShapeDtypeStruct(tab_shape, F32), jax.ShapeDtypeStruct(tab_shape, F32),
                   jax.ShapeDtypeStruct(tab_shape, BF16), jax.ShapeDtypeStruct(tab_shape, BF16)],
        compiler_params=_params("parallel"),
        name="route",
    )(mix_a, mix_b, h0, w_out_bf, ln_g, ln_b, wq_t_bf, k1_bf, k2_bf)


def _experts_kernel(wd0_ref, wd_ref, wut_ref, h1tbn_ref, h1t_ref, e1_ref, cn_ref, e2_ref, r2_ref,
                    g_ref, b_ref, o_ref, acc_ref, a0_ref, a1_ref, p_ref, *, ib, nk):
    j = pl.program_id(0)
    s = pl.program_id(1)
    tt = h1tbn_ref.shape[1]
    half = tt // 2

    @pl.when(s == 0)
    def _():
        acc_ref[...] = jnp.zeros_like(acc_ref)

    @pl.when((j == 0) & (s == 0))
    def _():
        a0_ref[...] = jnp.dot(wd0_ref[...], h1tbn_ref[...], preferred_element_type=F32)

    def step(a_cur, a_nxt):
        for hf in range(tt // half):
            hl = pl.ds(hf * half, half)
            for ii in range(ib):
                gsum = jnp.zeros((PEER_KEYS, half), BF16)
                for h in range(PEER_HEADS):
                    e1 = _bf(jnp.broadcast_to(e1_ref[h, ii:ii + 1, hl], gsum.shape))
                    cn = _bf(jnp.broadcast_to(cn_ref[h, ii:ii + 1, hl], gsum.shape))
                    val = e2_ref[h, :, hl] * e1
                    gsum = gsum + jnp.where(r2_ref[h, :, hl] < cn, val, jnp.zeros_like(val))
                rows = pl.ds(ii * PEER_KEYS, PEER_KEYS)
                p_ref[rows, hl] = _bf(_gelu(a_cur[rows, hl])) * gsum
            a_nxt[:, hl] = jnp.dot(wd_ref[...], h1tbn_ref[:, hl], preferred_element_type=F32)
            acc_ref[:, hl] += jnp.dot(wut_ref[...], p_ref[:, hl], preferred_element_type=F32)

    @pl.when(s % 2 == 0)
    def _():
        step(a0_ref, a1_ref)

    @pl.when(s % 2 == 1)
    def _():
        step(a1_ref, a0_ref)

    @pl.when(s == nk - 1)
    def _():
        y = DN_ALPHA * h1t_ref[...] + acc_ref[...]
        mu = jnp.mean(y, axis=0, keepdims=True)
        yc = y - mu
        var = jnp.mean(yc * yc, axis=0, keepdims=True)
        yn = yc * lax.rsqrt(var + LN_EPS)
        o_ref[...] = yn.T * g_ref[...] + b_ref[...]


def _experts(wd_bf, wu_bf, h1tb, h1t, e1, cn, e2, r2, ln_g, ln_b, *, tt, ib):
    d, n = h1t.shape
    nexp = wd_bf.shape[0]
    eb = ib * PEER_KEYS
    nk = nexp // eb
    wut_bf = wu_bf.reshape(nk, eb, d).transpose(0, 2, 1)
    last_tile = n // tt - 1
    tokcol = pl.BlockSpec((d, tt), lambda j, s: (0, j))
    tokcol_next = pl.BlockSpec((d, tt), lambda j, s: (0, jnp.minimum(j + (s + 1) // nk, last_tile)))
    tab_i = pl.BlockSpec((PEER_HEADS, ib, tt), lambda j, s: (0, s, j))
    tab_full = pl.BlockSpec((PEER_HEADS, PEER_KEYS, tt), lambda j, s: (0, 0, j))
    full = lambda a: pl.BlockSpec(a.shape, lambda j, s: (0,) * a.ndim)
    return pl.pallas_call(
        functools.partial(_experts_kernel, ib=ib, nk=nk),
        grid=(n // tt, nk),
        in_specs=[pl.BlockSpec((eb, d), lambda j, s: (0, 0)),
                  pl.BlockSpec((eb, d), lambda j, s: ((s + 1) % nk, 0)),
                  pl.BlockSpec((None, d, eb), lambda j, s: (s, 0, 0)),
                  tokcol_next, tokcol, tab_i, tab_i, tab_full, tab_full, full(ln_g), full(ln_b)],
        out_specs=pl.BlockSpec((tt, d), lambda j, s: (j, 0)),
        out_shape=jax.ShapeDtypeStruct((n, d), F32),
        scratch_shapes=[pltpu.VMEM((d, tt), F32),
                        pltpu.VMEM((eb, tt), F32), pltpu.VMEM((eb, tt), F32),
                        pltpu.VMEM((eb, tt), BF16)],
        compiler_params=_params("arbitrary", "arbitrary"),
        name="experts",
    )(wd_bf, wd_bf, wut_bf, h1tb, h1t, e1, cn, e2, r2, ln_g, ln_b)


class _Tiles(NamedTuple):
    inproj_rows: int
    hgrn_steps: int
    s5_steps: int
    route_tokens: int
    expert_tokens: int
    expert_first_keys: int


def _tiles(n, seq):
    pick = lambda size, preferred: preferred if size % preferred == 0 else size
    return _Tiles(inproj_rows=pick(n, 512), hgrn_steps=pick(seq, 512), s5_steps=pick(seq, 64),
                  route_tokens=pick(n, 512), expert_tokens=pick(n, 512), expert_first_keys=16)


def kernel(x, ln0_g, ln0_b, w_in, hg_lb_logits, hg_norm_g, s5_a_re, s5_a_im, s5_log_step,
           s5_b_re, s5_b_im, s5_c_re, s5_c_im, s5_d, w_glu, b_glu, w_out, ln1_g, ln1_b,
           w_query, peer_keys_1, peer_keys_2, peer_down, peer_up, ln2_g, ln2_b):
    assert w_in.shape[0] == 1, "single-layer trunk"
    batch, seq, d = x.shape
    n = batch * seq
    r2d = lambda a: a.reshape(1, -1).astype(F32)
    tiles = _tiles(n, seq)

    x2 = x.reshape(n, d).astype(F32)
    h0, q, gl, k, v, gate, u = _inproj(
        x2, r2d(ln0_g), r2d(ln0_b), _bf(w_in[0]), hg_lb_logits.astype(F32),
        tm=tiles.inproj_rows)

    mix_a = _hgrn(q, gl, k, v, gate, r2d(hg_norm_g[0]), batch=batch,
                  tc=tiles.hgrn_steps)

    abar_re, abar_im, wb, wc = _s5_discretise(
        s5_a_re[0].astype(F32), s5_a_im[0].astype(F32), s5_log_step[0],
        s5_b_re[0].astype(F32), s5_b_im[0].astype(F32), s5_c_re[0], s5_c_im[0])
    mix_b = _s5(u.reshape(batch, seq, -1), abar_re, abar_im, wb, wc, r2d(s5_d[0]),
                _bf(w_glu[0]), r2d(b_glu[0]), tt=tiles.s5_steps).reshape(n, -1)

    h1t, h1tb, e1, cn, e2, r2 = _route(
        mix_a, mix_b, h0, _bf(w_out[0]), r2d(ln1_g[0]), r2d(ln1_b[0]),
        _bf(w_query[0].T), _bf(peer_keys_1[0]), _bf(peer_keys_2[0]), tm=tiles.route_tokens)

    out = _experts(_bf(peer_down[0]), _bf(peer_up[0]), h1tb, h1t, e1, cn, e2, r2,
                   r2d(ln2_g[0]), r2d(ln2_b[0]), tt=tiles.expert_tokens,
                   ib=tiles.expert_first_keys)
    return out.reshape(batch, seq, d).astype(x.dtype)
```
